```python
import math, functools
import jax, jax.numpy as jnp
from jax import lax
import numpy as np

D_MODEL = 2048
BATCH = 2
SEQ = 4096
DEPTH = 1
DEC_BATCH = 8
DEC_SEQ = 8
PAST_LEN = 16384
PAGE_SIZE = 128

HEAD_DIM = 128
W_ATTN = D_MODEL // 2
N_HEADS = W_ATTN // HEAD_DIM
W_GMLP = D_MODEL - W_ATTN
N_GROUPS = 8
GROUP_DIM = W_GMLP // N_GROUPS
CHUNK = 128
IDX_HEADS = 16
IDX_DIM = 64
TOPK_MAX = 256
QBLK = 128
D_FF = ((8 * D_MODEL // 3 + 255) // 256) * 256
N_MOD = 9
EPS = 1e-6
IN_SIZES = (W_ATTN, W_ATTN, W_ATTN, IDX_HEADS * IDX_DIM, IDX_DIM, IDX_HEADS, W_GMLP, W_GMLP)
D_IN = sum(IN_SIZES)

kernel_name = "hymba_dsa_gmlp_macaron_step"


def rms_norm(x, g):
    xf = x.astype(jnp.float32)
    y = xf * lax.rsqrt(jnp.mean(xf * xf, axis=-1, keepdims=True) + EPS)
    return (y * g.astype(jnp.float32)).astype(x.dtype)


def alibi_slopes():
    return jnp.exp2(-8.0 * jnp.arange(1, N_HEADS + 1, dtype=jnp.float32) / N_HEADS)


def modulate(x, g, shift, scale):
    return rms_norm(x, g) * (1.0 + scale[:, None, :]) + shift[:, None, :]


def swiglu(x, wg, wu, wd):
    return (jax.nn.silu(x @ wg) * (x @ wu)) @ wd


def split_in(p):
    offs, acc = [], 0
    for s in IN_SIZES[:-1]:
        acc += s
        offs.append(acc)
    return jnp.split(p, offs, axis=-1)


gather_rows = jax.vmap(lambda rows, idx: rows[idx])


def indexer_scores(qi, wi, ki):
    dots = jnp.einsum('bthd,bsd->bths', qi, ki).astype(jnp.float32)
    return jnp.einsum('bths,bth->bts', jax.nn.relu(dots), wi.astype(jnp.float32))


def sparse_attend(q, k_sel, v_sel, q_pos, s_pos):
    logits = jnp.einsum('bthd,btkhd->bthk', q, k_sel).astype(jnp.float32) * (HEAD_DIM ** -0.5)
    dist = (q_pos[None, :, None] - s_pos).astype(jnp.float32)
    logits = logits - alibi_slopes()[None, None, :, None] * dist[:, :, None, :]
    valid = (s_pos <= q_pos[None, :, None])[:, :, None, :]
    logits = jnp.where(valid, logits, -jnp.inf)
    p = jax.nn.softmax(logits, axis=-1).astype(v_sel.dtype)
    out = jnp.einsum('bthk,btkhd->bthd', p, v_sel)
    return out.reshape(out.shape[0], out.shape[1], N_HEADS * HEAD_DIM)


def mixer_a_prompt(q, k, v, qi, ki, wi):
    B, S = q.shape[0], q.shape[1]
    topk = min(TOPK_MAX, S // 4)
    nb = S // QBLK
    key_pos = jnp.arange(S)

    def blk(args):
        j, qb, qib, wib = args
        q_pos = j * QBLK + jnp.arange(QBLK)
        sc = indexer_scores(qib, wib, ki)
        sc = jnp.where(key_pos[None, None, :] <= q_pos[None, :, None], sc, -jnp.inf)
        _, idx = lax.top_k(sc, topk)
        return sparse_attend(qb, gather_rows(k, idx), gather_rows(v, idx), q_pos, idx)

    def to_blocks(a):
        return jnp.swapaxes(a.reshape((B, nb, QBLK) + a.shape[2:]), 0, 1)

    out = lax.map(blk, (jnp.arange(nb), to_blocks(q), to_blocks(qi), to_blocks(wi)))
    return jnp.swapaxes(out, 0, 1).reshape(B, S, W_ATTN)


def mixer_a_sample(q, k, v, qi, ki, wi, cache_k, cache_v, cache_kidx, page_table):
    DB, T = q.shape[0], q.shape[1]
    past = page_table.shape[1] * PAGE_SIZE
    L = past + T
    topk = min(TOPK_MAX, L // 4)
    ki_past = cache_kidx[page_table].reshape(DB, past, IDX_DIM)
    ki_all = jnp.concatenate([ki_past, ki.astype(ki_past.dtype)], axis=1)
    q_pos = past + jnp.arange(T)
    sc = indexer_scores(qi, wi, ki_all)
    sc = jnp.where(jnp.arange(L)[None, None, :] <= q_pos[None, :, None], sc, -jnp.inf)
    _, idx = lax.top_k(sc, topk)
    is_past = (idx < past)[..., None, None]
    pidx = jnp.minimum(idx, past - 1)
    phys = jax.vmap(lambda pt, i: pt[i])(page_table, pidx // PAGE_SIZE)
    off = pidx % PAGE_SIZE
    nidx = jnp.clip(idx - past, 0, T - 1)
    k_sel = jnp.where(is_past, cache_k[phys, off].astype(k.dtype), gather_rows(k, nidx))
    v_sel = jnp.where(is_past, cache_v[phys, off].astype(v.dtype), gather_rows(v, nidx))
    return sparse_attend(q, k_sel, v_sel, q_pos, idx)


def mixer_b(u, vb, w_s, b_s):
    B, T = u.shape[0], u.shape[1]
    Tp = -(-T // CHUNK) * CHUNK
    nc = Tp // CHUNK
    vp = jnp.pad(vb, ((0, 0), (0, Tp - T), (0, 0), (0, 0))).reshape(B, nc, CHUNK, N_GROUPS, GROUP_DIM)
    w_m = jnp.tril(w_s)
    mixed = jnp.einsum('gts,bnsgc->bntgc', w_m, vp) + jnp.transpose(b_s)[None, None, :, :, None]
    mixed = mixed.reshape(B, Tp, N_GROUPS, GROUP_DIM)[:, :T]
    return (u * mixed).reshape(B, T, W_GMLP)


def layer(x, c, mixer_a, w_ada, b_ada, norm_ffn1, w1_gate, w1_up, w1_down, norm_mix, w_in,
          q_norm, k_norm, idx_k_norm, w_spatial, b_spatial, v_norm, out_norm_a, out_norm_b,
          w_out, norm_ffn2, w2_gate, w2_up, w2_down):
    B, T = x.shape[0], x.shape[1]
    mods = jnp.split(jax.nn.silu(c) @ w_ada + b_ada, N_MOD, axis=-1)
    sh1, sc1, g1, sh2, sc2, g2, sh3, sc3, g3 = mods
    h = x + 0.5 * g1[:, None, :] * swiglu(modulate(x, norm_ffn1, sh1, sc1), w1_gate, w1_up, w1_down)
    a = modulate(h, norm_mix, sh2, sc2)
    q, k, v, qi, ki, wi, u, vb = split_in(a @ w_in)
    q = rms_norm(q.reshape(B, T, N_HEADS, HEAD_DIM), q_norm)
    k = rms_norm(k.reshape(B, T, N_HEADS, HEAD_DIM), k_norm)
    v = v.reshape(B, T, N_HEADS, HEAD_DIM)
    qi = qi.reshape(B, T, IDX_HEADS, IDX_DIM) * (IDX_DIM ** -0.5)
    ki = rms_norm(ki, idx_k_norm)
    wi = wi * (IDX_HEADS ** -0.5)
    u = jax.nn.gelu(u).reshape(B, T, N_GROUPS, GROUP_DIM)
    vb = rms_norm(jax.nn.gelu(vb).reshape(B, T, N_GROUPS, GROUP_DIM), v_norm)
    out_a = mixer_a(q, k, v, qi, ki, wi)
    out_b = mixer_b(u, vb, w_spatial, b_spatial)
    mix = jnp.concatenate([rms_norm(out_a, out_norm_a), rms_norm(out_b, out_norm_b)], axis=-1) @ w_out
    h = h + g2[:, None, :] * mix
    y = h + 0.5 * g3[:, None, :] * swiglu(modulate(h, norm_ffn2, sh3, sc3), w2_gate, w2_up, w2_down)
    return y, k, v, ki, vb.reshape(B, T, W_GMLP)


def setup_inputs(seed: int = 0) -> dict:
    key = jax.random.key(seed)
    ks = jax.random.split(key, 40)
    n_pages = PAST_LEN // PAGE_SIZE
    n_used = DEC_BATCH * n_pages
    n_pool = n_used + max(1, n_used // 4)

    def nrm(k, shape, scale):
        return jax.random.normal(k, shape, jnp.float32) * scale

    def gain(k, shape):
        return 1.0 + 0.02 * jax.random.normal(k, shape, jnp.float32)

    D = D_MODEL
    page_table = jax.random.permutation(ks[5], n_pool)[:n_used].reshape(DEC_BATCH, n_pages).astype(jnp.int32)
    return {
        "x_prompt": nrm(ks[0], (BATCH, SEQ, D), 1.0),
        "x_sample": nrm(ks[1], (DEC_BATCH, DEC_SEQ, D), 1.0),
        "cache_k": nrm(ks[2], (DEPTH, n_pool, PAGE_SIZE, N_HEADS, HEAD_DIM), 1.0),
        "cache_v": nrm(ks[3], (DEPTH, n_pool, PAGE_SIZE, N_HEADS, HEAD_DIM), 1.0),
        "cache_kidx": nrm(ks[4], (DEPTH, n_pool, PAGE_SIZE, IDX_DIM), 1.0),
        "page_table": page_table,
        "c_prompt": nrm(ks[6], (BATCH, D), 1.0),
        "c_sample": nrm(ks[7], (DEC_BATCH, D), 1.0),
        "w_ada": nrm(ks[8], (DEPTH, D, N_MOD * D), 0.5 * D ** -0.5),
        "b_ada": nrm(ks[9], (DEPTH, N_MOD * D), 0.01),
        "norm_ffn1": gain(ks[10], (DEPTH, D)),
        "w1_gate": nrm(ks[11], (DEPTH, D, D_FF), D ** -0.5),
        "w1_up": nrm(ks[12], (DEPTH, D, D_FF), D ** -0.5),
        "w1_down": nrm(ks[13], (DEPTH, D_FF, D), D_FF ** -0.5),
        "norm_mix": gain(ks[14], (DEPTH, D)),
        "w_in": nrm(ks[15], (DEPTH, D, D_IN), D ** -0.5),
        "q_norm": gain(ks[16], (DEPTH, HEAD_DIM)),
        "k_norm": gain(ks[17], (DEPTH, HEAD_DIM)),
        "idx_k_norm": gain(ks[18], (DEPTH, IDX_DIM)),
        "w_spatial": nrm(ks[19], (DEPTH, N_GROUPS, CHUNK, CHUNK), CHUNK ** -0.5),
        "b_spatial": gain(ks[20], (DEPTH, N_GROUPS, CHUNK)),
        "v_norm": gain(ks[21], (DEPTH, N_GROUPS, GROUP_DIM)),
        "out_norm_a": gain(ks[22], (DEPTH, W_ATTN)),
        "out_norm_b": gain(ks[23], (DEPTH, W_GMLP)),
        "w_out": nrm(ks[24], (DEPTH, D, D), D ** -0.5),
        "norm_ffn2": gain(ks[25], (DEPTH, D)),
        "w2_gate": nrm(ks[26], (DEPTH, D, D_FF), D ** -0.5),
        "w2_up": nrm(ks[27], (DEPTH, D, D_FF), D ** -0.5),
        "w2_down": nrm(ks[28], (DEPTH, D_FF, D), D_FF ** -0.5),
    }


def reference(x_prompt, x_sample, cache_k, cache_v, cache_kidx, page_table, c_prompt, c_sample,
              w_ada, b_ada, norm_ffn1, w1_gate, w1_up, w1_down, norm_mix, w_in, q_norm, k_norm,
              idx_k_norm, w_spatial, b_spatial, v_norm, out_norm_a, out_norm_b, w_out, norm_ffn2,
              w2_gate, w2_up, w2_down):
    weights = (w_ada, b_ada, norm_ffn1, w1_gate, w1_up, w1_down, norm_mix, w_in, q_norm, k_norm,
               idx_k_norm, w_spatial, b_spatial, v_norm, out_norm_a, out_norm_b, w_out, norm_ffn2,
               w2_gate, w2_up, w2_down)
    yp, ys = x_prompt, x_sample
    kp_l, vp_l, kip_l, ks_l, vs_l, kis_l, vch_l = [], [], [], [], [], [], []
    for l in range(DEPTH):
        lw = [w[l] for w in weights]
        yp, kp, vp, kip, _ = layer(yp, c_prompt, mixer_a_prompt, *lw)
        mix_s = functools.partial(mixer_a_sample, cache_k=cache_k[l], cache_v=cache_v[l],
                                  cache_kidx=cache_kidx[l], page_table=page_table)
        ys, k_s, v_s, ki_s, vch_s = layer(ys, c_sample, mix_s, *lw)
        kp_l.append(kp); vp_l.append(vp); kip_l.append(kip)
        ks_l.append(k_s); vs_l.append(v_s); kis_l.append(ki_s); vch_l.append(vch_s)
    k_prompt = jnp.stack(kp_l, 0)
    v_prompt = jnp.stack(vp_l, 0)
    kidx_prompt = jnp.stack(kip_l, 0)
    k_sample = jnp.stack(ks_l, 0)
    v_sample = jnp.stack(vs_l, 0)
    kidx_sample = jnp.stack(kis_l, 0)
    vchunk_sample = jnp.stack(vch_l, 0)
    return (yp, ys, k_prompt, v_prompt, kidx_prompt, k_sample, v_sample, kidx_sample, vchunk_sample)
```

```python
import functools
import math

import jax
import jax.numpy as jnp
from jax import lax
from jax.experimental import pallas as pl
from jax.experimental.pallas import tpu as pltpu

F32 = jnp.float32
BF16 = jnp.bfloat16
I32 = jnp.int32

EPS = 1e-6
HEAD_DIM = 128
N_HEADS = 8
N_GROUPS = 8
GROUP_DIM = 128
CHUNK = 128
IDX_HEADS = 16
IDX_DIM = 64
TOPK_MAX = 256
PAGE_SIZE = 128
N_MOD = 9
DEC_ROWS = 8

INT_MIN = -(2 ** 31)
INT_MAX = 2 ** 31 - 1
F32_MIN = float(jnp.finfo(jnp.float32).min)

V7X_VMEM_BYTES = 64 * 1024 * 1024
VMEM_LIMIT_BYTES = 60000 * 1024

NT_DIMS = (((1,), (1,)), ((), ()))


def _compiler_params(semantics):
    return pltpu.CompilerParams(dimension_semantics=semantics, vmem_limit_bytes=VMEM_LIMIT_BYTES)


def _modulate(x, gain, shift, scale):
    ms = jnp.mean(x * x, axis=-1, keepdims=True)
    y = x * lax.rsqrt(ms + EPS)
    return (y * gain) * (1.0 + scale) + shift


def _rms_rows(x, gain):
    ms = jnp.mean(x * x, axis=-1, keepdims=True)
    return (x * lax.rsqrt(ms + EPS)) * gain


def _rms_cols(x, gain_col):
    ms = jnp.mean(x * x, axis=0, keepdims=True)
    return (x * lax.rsqrt(ms + EPS)) * gain_col


def _sortable_key(x):
    bits = lax.bitcast_convert_type(x, I32)
    return jnp.where(bits < 0, bits ^ jnp.int32(INT_MAX), bits)


def _ada_kernel(c_ref, w_ref, b_ref, o_ref):
    a = jax.nn.silu(c_ref[...]).astype(BF16)
    o_ref[...] = jnp.dot(a, w_ref[...].astype(BF16), preferred_element_type=F32) + b_ref[...]


def _ada_call(c_all, w_ada, b_ada):
    rows, d = c_all.shape
    n = w_ada.shape[1]
    tn = 1024
    return pl.pallas_call(
        _ada_kernel,
        grid=(n // tn,),
        in_specs=[
            pl.BlockSpec((rows, d), lambda j: (0, 0)),
            pl.BlockSpec((d, tn), lambda j: (0, j)),
            pl.BlockSpec((1, tn), lambda j: (0, j)),
        ],
        out_specs=pl.BlockSpec((rows, tn), lambda j: (0, j)),
        out_shape=jax.ShapeDtypeStruct((rows, n), F32),
        compiler_params=_compiler_params(("parallel",)),
        name="ada_proj",
    )(c_all, w_ada, b_ada.reshape(1, n))


def _ffn_kernel(x_ref, sh_ref, sc_ref, g_ref, nw_ref, wg_ref, wu_ref, wd_ref, o_ref, xn_ref, acc_ref):
    n = pl.program_id(1)

    @pl.when(n == 0)
    def _():
        xn_ref[...] = _modulate(x_ref[...], nw_ref[...], sh_ref[...], sc_ref[...]).astype(BF16)
        acc_ref[...] = jnp.zeros_like(acc_ref)

    xn = xn_ref[...]
    gate = jnp.dot(xn, wg_ref[...], preferred_element_type=F32)
    up = jnp.dot(xn, wu_ref[...], preferred_element_type=F32)
    hmid = (jax.nn.silu(gate) * up).astype(BF16)
    acc_ref[...] += jnp.dot(hmid, wd_ref[...], preferred_element_type=F32)

    @pl.when(n == pl.num_programs(1) - 1)
    def _():
        o_ref[...] = x_ref[...] + (0.5 * g_ref[...]) * acc_ref[...]


def _mod_spec(mod, tiles_per_group):
    _, r, d = mod.shape
    return pl.BlockSpec((None, r, d), lambda i, *_: (i // tiles_per_group, 0, 0))


def _ffn_call(x, shift, scale, gate, norm_w, wg, wu, wd, *, tm, tiles_per_group):
    m, d = x.shape
    dff = wg.shape[1]
    tn = 512
    return pl.pallas_call(
        _ffn_kernel,
        grid=(m // tm, dff // tn),
        in_specs=[
            pl.BlockSpec((tm, d), lambda i, n: (i, 0)),
            _mod_spec(shift, tiles_per_group),
            _mod_spec(scale, tiles_per_group),
            _mod_spec(gate, tiles_per_group),
            pl.BlockSpec((1, d), lambda i, n: (0, 0)),
            pl.BlockSpec((d, tn), lambda i, n: (0, n)),
            pl.BlockSpec((d, tn), lambda i, n: (0, n)),
            pl.BlockSpec((tn, d), lambda i, n: (n, 0)),
        ],
        out_specs=pl.BlockSpec((tm, d), lambda i, n: (i, 0)),
        out_shape=jax.ShapeDtypeStruct((m, d), F32),
        scratch_shapes=[pltpu.VMEM((tm, d), BF16), pltpu.VMEM((tm, d), F32)],
        compiler_params=_compiler_params(("parallel", "arbitrary")),
        name="ffn",
    )(x, shift, scale, gate, norm_w.reshape(1, d), wg, wu, wd)


def _inproj_kernel(h_ref, sh_ref, sc_ref, nw_ref, wn_ref, wt_ref, qn_ref, kn_ref, ikn_ref, vn_ref,
                   qT_ref, qiT_ref, wT_ref, vT_ref, k_ref, kb_ref, v_ref, ki_ref, kib_ref, u_ref, vb_ref):
    w_attn = N_HEADS * HEAD_DIM
    w_idx = IDX_HEADS * IDX_DIM
    a = _modulate(h_ref[...], nw_ref[...], sh_ref[...], sc_ref[...]).astype(BF16)

    q_t = lax.dot_general(wt_ref[0:w_attn, :], a, NT_DIMS, preferred_element_type=F32)
    for h in range(N_HEADS):
        rows = slice(h * HEAD_DIM, (h + 1) * HEAD_DIM)
        qT_ref[rows, :] = _rms_cols(q_t[rows, :], qn_ref[...]).astype(BF16)
    qi_t = lax.dot_general(wt_ref[w_attn:w_attn + w_idx, :], a, NT_DIMS, preferred_element_type=F32)
    qiT_ref[...] = (qi_t * (IDX_DIM ** -0.5)).astype(BF16)
    v_t = lax.dot_general(wt_ref[w_attn + w_idx:2 * w_attn + w_idx, :], a, NT_DIMS, preferred_element_type=F32)
    vT_ref[...] = v_t.astype(BF16)
    w_t = lax.dot_general(wt_ref[2 * w_attn + w_idx:2 * w_attn + w_idx + IDX_HEADS, :], a, NT_DIMS,
                          preferred_element_type=F32)
    wT_ref[...] = w_t * (IDX_HEADS ** -0.5)

    k = jnp.dot(a, wn_ref[:, 0:w_attn], preferred_element_type=F32)
    for h in range(N_HEADS):
        cols = slice(h * HEAD_DIM, (h + 1) * HEAD_DIM)
        kh = _rms_rows(k[:, cols], kn_ref[...])
        k_ref[:, cols] = kh
        kb_ref[:, cols] = kh.astype(BF16)
    v_ref[...] = jnp.dot(a, wn_ref[:, w_attn:2 * w_attn], preferred_element_type=F32)
    u_ref[...] = jax.nn.gelu(jnp.dot(a, wn_ref[:, 2 * w_attn:3 * w_attn], preferred_element_type=F32))
    vb = jax.nn.gelu(jnp.dot(a, wn_ref[:, 3 * w_attn:4 * w_attn], preferred_element_type=F32))
    for g in range(N_GROUPS):
        cols = slice(g * GROUP_DIM, (g + 1) * GROUP_DIM)
        vb_ref[:, cols] = _rms_rows(vb[:, cols], vn_ref[g:g + 1, :])
    kiw = jnp.dot(a, wn_ref[:, 4 * w_attn:4 * w_attn + 128], preferred_element_type=F32)
    ki = _rms_rows(kiw[:, 0:IDX_DIM], ikn_ref[...])
    ki_ref[...] = ki
    kib_ref[...] = ki.astype(BF16)


def _inproj_call(h, shift, scale, norm_w, wn, wt, q_norm, k_norm, idx_k_norm, v_norm, *, tm, tiles_per_group):
    m, d = h.shape
    nt = m // tm
    w_attn = N_HEADS * HEAD_DIM
    w_idx = IDX_HEADS * IDX_DIM
    w_g = N_GROUPS * GROUP_DIM
    row = lambda i: (i, 0)
    fm = lambda i: (i, 0, 0)
    const = lambda i: (0, 0)
    out_shapes = [
        jax.ShapeDtypeStruct((nt, w_attn, tm), BF16),
        jax.ShapeDtypeStruct((nt, w_idx, tm), BF16),
        jax.ShapeDtypeStruct((nt, IDX_HEADS, tm), F32),
        jax.ShapeDtypeStruct((nt, w_attn, tm), BF16),
        jax.ShapeDtypeStruct((m, w_attn), F32),
        jax.ShapeDtypeStruct((m, w_attn), BF16),
        jax.ShapeDtypeStruct((m, w_attn), F32),
        jax.ShapeDtypeStruct((m, IDX_DIM), F32),
        jax.ShapeDtypeStruct((m, IDX_DIM), BF16),
        jax.ShapeDtypeStruct((m, w_g), F32),
        jax.ShapeDtypeStruct((m, w_g), F32),
    ]
    out_specs = [
        pl.BlockSpec((None, w_attn, tm), fm),
        pl.BlockSpec((None, w_idx, tm), fm),
        pl.BlockSpec((None, IDX_HEADS, tm), fm),
        pl.BlockSpec((None, w_attn, tm), fm),
        pl.BlockSpec((tm, w_attn), row),
        pl.BlockSpec((tm, w_attn), row),
        pl.BlockSpec((tm, w_attn), row),
        pl.BlockSpec((tm, IDX_DIM), row),
        pl.BlockSpec((tm, IDX_DIM), row),
        pl.BlockSpec((tm, w_g), row),
        pl.BlockSpec((tm, w_g), row),
    ]
    return pl.pallas_call(
        _inproj_kernel,
        grid=(nt,),
        in_specs=[
            pl.BlockSpec((tm, d), row),
            _mod_spec(shift, tiles_per_group),
            _mod_spec(scale, tiles_per_group),
            pl.BlockSpec((1, d), const),
            pl.BlockSpec(wn.shape, const),
            pl.BlockSpec(wt.shape, const),
            pl.BlockSpec((HEAD_DIM, 1), const),
            pl.BlockSpec((1, HEAD_DIM), const),
            pl.BlockSpec((1, IDX_DIM), const),
            pl.BlockSpec((N_GROUPS, GROUP_DIM), const),
        ],
        out_specs=out_specs,
        out_shape=out_shapes,
        compiler_params=_compiler_params(("parallel",)),
        name="in_proj",
    )(h, shift, scale, norm_w.reshape(1, d), wn, wt, q_norm.reshape(HEAD_DIM, 1), k_norm.reshape(1, HEAD_DIM),
      idx_k_norm.reshape(1, IDX_DIM), v_norm)


def _topk_threshold(count_ge, topk, like):
    zero = jnp.zeros_like(like)
    cnt0 = count_ge(zero)
    nonneg = cnt0 >= topk
    t0 = jnp.where(nonneg, zero, jnp.int32(INT_MIN))
    c0 = jnp.where(nonneg, cnt0, jnp.int32(topk))

    def body(p, carry):
        t, ct = carry
        cand = t | jnp.left_shift(jnp.int32(1), 31 - p)
        cnt = count_ge(cand)
        take = cnt >= topk
        return jnp.where(take, cand, t), jnp.where(take, cnt, ct)

    return lax.fori_loop(1, 32, body, (t0, c0))


def _tie_cutoff(count_where, thr, topk, nbits, excess):
    need = topk - count_where(lambda key, pos: key > thr)
    j = jnp.zeros_like(thr)
    for bit in reversed(range(nbits)):
        cand = j | jnp.int32(1 << bit)
        cnt = count_where(lambda key, pos, cand=cand: (key == thr) & (pos < cand))
        j = jnp.where(cnt < need, cand, j)
    return jnp.where(excess, j, jnp.int32(INT_MAX))


def _attn_prompt_kernel(qT_ref, qiT_ref, wT_ref, kb_ref, vT_ref, kib_ref, o_ref,
                        key_ref, dm_ref, acc_ref, m_ref, l_ref, *, tq, topk, pos_bits):
    j = pl.program_id(1)
    nchunks = j + 1
    scale = HEAD_DIM ** -0.5
    t_pos = j * tq + lax.broadcasted_iota(I32, (1, tq), 1)
    s_off = lax.broadcasted_iota(I32, (tq, 1), 0)

    def chunk_start(c):
        return pl.multiple_of(c * tq, tq)

    def score_body(c, carry):
        off = chunk_start(c)
        kic = kib_ref[pl.ds(off, tq), :]
        sc = jnp.zeros((tq, tq), F32)
        for h in range(IDX_HEADS):
            d = jnp.dot(kic, qiT_ref[h * IDX_DIM:(h + 1) * IDX_DIM, :], preferred_element_type=F32)
            sc = sc + jnp.maximum(d, 0.0) * wT_ref[h:h + 1, :]
        valid = (off + s_off) <= t_pos
        key_ref[pl.ds(off, tq), :] = jnp.where(valid, _sortable_key(sc), jnp.int32(INT_MIN))
        return carry

    lax.fori_loop(0, nchunks, score_body, 0)

    def count_where(pred):
        def body(c, cnt):
            off = chunk_start(c)
            hit = pred(key_ref[pl.ds(off, tq), :], off + s_off).astype(I32)
            return cnt + jnp.sum(hit.reshape(tq // 8, 8, tq), axis=0)

        cnt8 = lax.fori_loop(0, nchunks, body, jnp.zeros((8, tq), I32))
        return jnp.sum(cnt8, axis=0, keepdims=True)

    thr, cnt_thr = _topk_threshold(lambda cand: count_where(lambda key, pos: key >= cand), topk, t_pos)
    excess = (cnt_thr > topk) & (thr != jnp.int32(INT_MIN))
    cutoff = lax.cond(
        jnp.max(excess.astype(I32)) > 0,
        lambda: _tie_cutoff(count_where, thr, topk, pos_bits, excess),
        lambda: jnp.full((1, tq), INT_MAX, I32),
    )

    def dm_body(c, carry):
        off = chunk_start(c)
        key = key_ref[pl.ds(off, tq), :]
        pos = off + s_off
        sel = ((key > thr) | ((key == thr) & (pos <= cutoff))) & (pos <= t_pos)
        dm_ref[pl.ds(off, tq), :] = jnp.where(sel, (t_pos - pos).astype(F32), jnp.inf)
        return carry

    lax.fori_loop(0, nchunks, dm_body, 0)

    m_ref[...] = jnp.full(m_ref.shape, F32_MIN, F32)
    l_ref[...] = jnp.zeros_like(l_ref)
    acc_ref[...] = jnp.zeros_like(acc_ref)

    def att_body(c, carry):
        off = chunk_start(c)
        dmc = dm_ref[pl.ds(off, tq), :]
        for h in range(N_HEADS):
            rows = slice(h * HEAD_DIM, (h + 1) * HEAD_DIM)
            kc = kb_ref[pl.ds(off, tq), rows]
            s = jnp.dot(kc, qT_ref[rows, :], preferred_element_type=F32) * scale - (2.0 ** -(h + 1)) * dmc
            m_old = m_ref[h:h + 1, :]
            m_new = jnp.maximum(m_old, jnp.max(s, axis=0, keepdims=True))
            alpha = jnp.exp(m_old - m_new)
            p = jnp.exp(s - m_new)
            l_ref[h:h + 1, :] = alpha * l_ref[h:h + 1, :] + jnp.sum(p, axis=0, keepdims=True)
            pv = jnp.dot(vT_ref[c, rows, :], p.astype(BF16), preferred_element_type=F32)
            acc_ref[rows, :] = alpha * acc_ref[rows, :] + pv
            m_ref[h:h + 1, :] = m_new
        return carry

    lax.fori_loop(0, nchunks, att_body, 0)

    for h in range(N_HEADS):
        rows = slice(h * HEAD_DIM, (h + 1) * HEAD_DIM)
        acc_ref[rows, :] = acc_ref[rows, :] / l_ref[h:h + 1, :]
    o_ref[...] = acc_ref[...].T


def _attn_prompt_call(qT, qiT, wT, kb, vT, kib, *, batch, seq, tq):
    w_attn = N_HEADS * HEAD_DIM
    w_idx = IDX_HEADS * IDX_DIM
    nq = seq // tq
    topk = min(TOPK_MAX, seq // 4)
    qblk = lambda b, j: (b * nq + j, 0, 0)
    kernel = functools.partial(_attn_prompt_kernel, tq=tq, topk=topk, pos_bits=max(1, (seq - 1).bit_length()))
    return pl.pallas_call(
        kernel,
        grid=(batch, nq),
        in_specs=[
            pl.BlockSpec((None, w_attn, tq), qblk),
            pl.BlockSpec((None, w_idx, tq), qblk),
            pl.BlockSpec((None, IDX_HEADS, tq), qblk),
            pl.BlockSpec((seq, w_attn), lambda b, j: (b, 0), pipeline_mode=pl.Buffered(1)),
            pl.BlockSpec((nq, w_attn, tq), lambda b, j: (b, 0, 0), pipeline_mode=pl.Buffered(1)),
            pl.BlockSpec((seq, IDX_DIM), lambda b, j: (b, 0), pipeline_mode=pl.Buffered(1)),
        ],
        out_specs=pl.BlockSpec((tq, w_attn), lambda b, j: (b * nq + j, 0)),
        out_shape=jax.ShapeDtypeStruct((batch * seq, w_attn), F32),
        scratch_shapes=[
            pltpu.VMEM((seq, tq), I32),
            pltpu.VMEM((seq, tq), F32),
            pltpu.VMEM((w_attn, tq), F32),
            pltpu.VMEM((N_HEADS, tq), F32),
            pltpu.VMEM((N_HEADS, tq), F32),
        ],
        compiler_params=_compiler_params(("parallel", "arbitrary")),
        name="attn_prompt",
    )(qT, qiT, wT, kb, vT, kib)


def _outproj_kernel(oa_ref, u_ref, vb_ref, h_ref, g_ref, ws_ref, bs_ref, na_ref, nb_ref, wo_ref, o_ref, *, unit):
    tm = oa_ref.shape[0]
    w_attn = N_HEADS * HEAD_DIM
    rows_i = lax.broadcasted_iota(I32, (unit, unit), 0)
    cols_i = lax.broadcasted_iota(I32, (unit, unit), 1)
    causal = rows_i >= cols_i
    ob_parts = []
    for g in range(N_GROUPS):
        cols = slice(g * GROUP_DIM, (g + 1) * GROUP_DIM)
        wm = jnp.where(causal, ws_ref[g], 0.0).astype(BF16)
        bias = bs_ref[:, g:g + 1]
        mixed = []
        for r in range(tm // unit):
            rws = slice(r * unit, (r + 1) * unit)
            mixed.append(jnp.dot(wm, vb_ref[rws, cols].astype(BF16), preferred_element_type=F32) + bias)
        mixed = mixed[0] if len(mixed) == 1 else jnp.concatenate(mixed, axis=0)
        ob_parts.append(u_ref[:, cols] * mixed)
    ob = jnp.concatenate(ob_parts, axis=1)
    za = _rms_rows(oa_ref[...], na_ref[...]).astype(BF16)
    zb = _rms_rows(ob, nb_ref[...]).astype(BF16)
    mix = (jnp.dot(za, wo_ref[0:w_attn, :], preferred_element_type=F32)
           + jnp.dot(zb, wo_ref[w_attn:, :], preferred_element_type=F32))
    o_ref[...] = h_ref[...] + g_ref[...] * mix


def _outproj_call(out_a, u, vb, h, gate, ws, bs_t, norm_a, norm_b, w_out, *, tm, tiles_per_group, unit):
    m, d = h.shape
    w_attn = out_a.shape[1]
    w_g = u.shape[1]
    row = lambda i: (i, 0)
    const = lambda i: (0, 0)
    return pl.pallas_call(
        functools.partial(_outproj_kernel, unit=unit),
        grid=(m // tm,),
        in_specs=[
            pl.BlockSpec((tm, w_attn), row),
            pl.BlockSpec((tm, w_g), row),
            pl.BlockSpec((tm, w_g), row),
            pl.BlockSpec((tm, d), row),
            _mod_spec(gate, tiles_per_group),
            pl.BlockSpec(ws.shape, lambda i: (0, 0, 0)),
            pl.BlockSpec(bs_t.shape, const),
            pl.BlockSpec((1, w_attn), const),
            pl.BlockSpec((1, w_g), const),
            pl.BlockSpec(w_out.shape, const),
        ],
        out_specs=pl.BlockSpec((tm, d), row),
        out_shape=jax.ShapeDtypeStruct((m, d), F32),
        compiler_params=_compiler_params(("parallel",)),
        name="out_proj",
    )(out_a, u, vb, h, gate, ws, bs_t, norm_a.reshape(1, w_attn), norm_b.reshape(1, w_g), w_out)


def _select_decode_kernel(pt_ref, qi_ref, w_ref, kin_ref, *rest, pages_per_step, n_pages, topk, pos_bits):
    page_refs = rest[:pages_per_step]
    dm_ref = rest[pages_per_step]
    key_ref = rest[pages_per_step + 1]
    p = pl.program_id(1)
    qi = qi_ref[...]
    w = w_ref[...]

    def page_scores(keys_bf16):
        d = lax.dot_general(qi, keys_bf16, NT_DIMS, preferred_element_type=F32)
        r = jnp.maximum(d, 0.0) * w
        return jnp.sum(r.reshape(IDX_HEADS, DEC_ROWS, PAGE_SIZE), axis=0)

    for i in range(pages_per_step):
        key_ref[p * pages_per_step + i] = _sortable_key(page_scores(page_refs[i][...].astype(BF16)))

    @pl.when(p == pl.num_programs(1) - 1)
    def _():
        tok = lax.broadcasted_iota(I32, (DEC_ROWS, PAGE_SIZE), 0)
        lane = lax.broadcasted_iota(I32, (DEC_ROWS, PAGE_SIZE), 1)
        new_valid = lane <= tok
        key_ref[n_pages] = jnp.where(new_valid, _sortable_key(page_scores(kin_ref[...])), jnp.int32(INT_MIN))

        pos = (lax.broadcasted_iota(I32, (n_pages + 1, DEC_ROWS, PAGE_SIZE), 0) * PAGE_SIZE
               + lax.broadcasted_iota(I32, (n_pages + 1, DEC_ROWS, PAGE_SIZE), 2))

        def count_where(pred):
            hit = pred(key_ref[...], pos).astype(I32)
            return jnp.sum(jnp.sum(hit, axis=0), axis=-1, keepdims=True)

        like = jnp.zeros((DEC_ROWS, 1), I32)
        thr, cnt_thr = _topk_threshold(lambda cand: count_where(lambda key, ps: key >= cand), topk, like)
        excess = (cnt_thr > topk) & (thr != jnp.int32(INT_MIN))
        cutoff = lax.cond(
            jnp.max(excess.astype(I32)) > 0,
            lambda: _tie_cutoff(count_where, thr, topk, pos_bits, excess),
            lambda: jnp.full((DEC_ROWS, 1), INT_MAX, I32),
        )
        key = key_ref[...]
        q_pos = n_pages * PAGE_SIZE + lax.broadcasted_iota(I32, (n_pages + 1, DEC_ROWS, PAGE_SIZE), 1)
        sel = ((key > thr) | ((key == thr) & (pos <= cutoff))) & (pos <= q_pos)
        dm_ref[...] = jnp.where(sel, (q_pos - pos).astype(F32), jnp.inf)


def _select_decode_call(page_table, qi2d, wcol, ki_new, cache_kidx, *, pages_per_step):
    nseq, n_pages = page_table.shape
    topk = min(TOPK_MAX, (n_pages * PAGE_SIZE + DEC_ROWS) // 4)
    pos_bits = max(1, ((n_pages + 1) * PAGE_SIZE - 1).bit_length())
    per_seq = lambda b, p, pt: (b, 0, 0)

    def page_spec(i):
        return pl.BlockSpec((None, PAGE_SIZE, IDX_DIM),
                            lambda b, p, pt, i=i: (pt[b, p * pages_per_step + i], 0, 0))

    kernel = functools.partial(_select_decode_kernel, pages_per_step=pages_per_step, n_pages=n_pages,
                               topk=topk, pos_bits=pos_bits)
    grid_spec = pltpu.PrefetchScalarGridSpec(
        num_scalar_prefetch=1,
        grid=(nseq, n_pages // pages_per_step),
        in_specs=[
            pl.BlockSpec((None, IDX_HEADS * DEC_ROWS, IDX_DIM), per_seq),
            pl.BlockSpec((None, IDX_HEADS * DEC_ROWS, 1), per_seq),
            pl.BlockSpec((None, PAGE_SIZE, IDX_DIM), per_seq),
        ] + [page_spec(i) for i in range(pages_per_step)],
        out_specs=pl.BlockSpec((None, n_pages + 1, DEC_ROWS, PAGE_SIZE), lambda b, p, pt: (b, 0, 0, 0)),
        scratch_shapes=[pltpu.VMEM((n_pages + 1, DEC_ROWS, PAGE_SIZE), I32)],
    )
    return pl.pallas_call(
        kernel,
        grid_spec=grid_spec,
        out_shape=jax.ShapeDtypeStruct((nseq, n_pages + 1, DEC_ROWS, PAGE_SIZE), F32),
        compiler_params=_compiler_params(("parallel", "arbitrary")),
        name="select_decode",
    )(page_table, qi2d, wcol, ki_new, *([cache_kidx] * pages_per_step))


def _attn_decode_kernel(pt_ref, q_ref, slope_ref, dm_ref, kn_ref, vn_ref, *rest, pages_per_step, n_pages):
    k_refs = rest[:pages_per_step]
    v_refs = rest[pages_per_step:2 * pages_per_step]
    o_ref, acc_ref, m_ref, l_ref = rest[2 * pages_per_step:]
    p = pl.program_id(1)
    scale = HEAD_DIM ** -0.5

    @pl.when(p == 0)
    def _():
        m_ref[...] = jnp.full(m_ref.shape, F32_MIN, F32)
        l_ref[...] = jnp.zeros_like(l_ref)
        acc_ref[...] = jnp.zeros_like(acc_ref)

    def update(keys_bf16, vals_bf16, dm_page):
        s = lax.dot_general(q_ref[...], keys_bf16, NT_DIMS, preferred_element_type=F32)
        s = s * scale - slope_ref[...] * jnp.concatenate([dm_page] * N_HEADS, axis=0)
        m_old = m_ref[...]
        m_new = jnp.maximum(m_old, jnp.max(s, axis=-1, keepdims=True))
        alpha = jnp.exp(m_old - m_new)
        pr = jnp.exp(s - m_new)
        l_ref[...] = alpha * l_ref[...] + jnp.sum(pr, axis=-1, keepdims=True)
        acc_ref[...] = alpha * acc_ref[...] + jnp.dot(pr.astype(BF16), vals_bf16, preferred_element_type=F32)
        m_ref[...] = m_new

    for i in range(pages_per_step):
        update(k_refs[i][...].astype(BF16), v_refs[i][...].astype(BF16), dm_ref[p * pages_per_step + i])

    @pl.when(p == pl.num_programs(1) - 1)
    def _():
        update(kn_ref[...], vn_ref[...], dm_ref[n_pages])
        parts = []
        for h in range(N_HEADS):
            rows = slice(h * DEC_ROWS, (h + 1) * DEC_ROWS)
            parts.append(acc_ref[rows, h * HEAD_DIM:(h + 1) * HEAD_DIM] / l_ref[rows, :])
        o_ref[...] = jnp.concatenate(parts, axis=1)


def _attn_decode_call(page_table, q_bd, slope_col, dm, k_new, v_new, cache_k, cache_v, *, pages_per_step):
    nseq, n_pages = page_table.shape
    w_attn = N_HEADS * HEAD_DIM
    rows = N_HEADS * DEC_ROWS
    per_seq = lambda b, p, pt: (b, 0, 0)

    def page_spec(i):
        return pl.BlockSpec((None, PAGE_SIZE, w_attn),
                            lambda b, p, pt, i=i: (pt[b, p * pages_per_step + i], 0, 0))

    kernel = functools.partial(_attn_decode_kernel, pages_per_step=pages_per_step, n_pages=n_pages)
    grid_spec = pltpu.PrefetchScalarGridSpec(
        num_scalar_prefetch=1,
        grid=(nseq, n_pages // pages_per_step),
        in_specs=[
            pl.BlockSpec((None, rows, w_attn), per_seq),
            pl.BlockSpec((rows, 1), lambda b, p, pt: (0, 0)),
            pl.BlockSpec((None, n_pages + 1, DEC_ROWS, PAGE_SIZE), lambda b, p, pt: (b, 0, 0, 0)),
            pl.BlockSpec((None, PAGE_SIZE, w_attn), per_seq),
            pl.BlockSpec((None, PAGE_SIZE, w_attn), per_seq),
        ] + [page_spec(i) for i in range(pages_per_step)] + [page_spec(i) for i in range(pages_per_step)],
        out_specs=pl.BlockSpec((None, DEC_ROWS, w_attn), per_seq),
        scratch_shapes=[
            pltpu.VMEM((rows, w_attn), F32),
            pltpu.VMEM((rows, 1), F32),
            pltpu.VMEM((rows, 1), F32),
        ],
    )
    return pl.pallas_call(
        kernel,
        grid_spec=grid_spec,
        out_shape=jax.ShapeDtypeStruct((nseq, DEC_ROWS, w_attn), F32),
        compiler_params=_compiler_params(("parallel", "arbitrary")),
        name="attn_decode",
    )(page_table, q_bd, slope_col, dm, k_new, v_new, *([cache_k] * pages_per_step), *([cache_v] * pages_per_step))


def _pack_in_weights(w_in):
    w_attn = N_HEADS * HEAD_DIM
    w_idx = IDX_HEADS * IDX_DIM
    w_g = N_GROUPS * GROUP_DIM
    o_q, o_k, o_v = 0, w_attn, 2 * w_attn
    o_qi = 3 * w_attn
    o_ki = o_qi + w_idx
    o_wi = o_ki + IDX_DIM
    o_u = o_wi + IDX_HEADS
    o_vb = o_u + w_g
    d = w_in.shape[0]
    pad = jnp.zeros((d, 128 - IDX_DIM), w_in.dtype)
    wn = jnp.concatenate([w_in[:, o_k:o_v], w_in[:, o_v:o_qi], w_in[:, o_u:o_vb], w_in[:, o_vb:o_vb + w_g],
                          w_in[:, o_ki:o_wi], pad], axis=1).astype(BF16)
    wt = jnp.concatenate([w_in[:, o_q:o_k], w_in[:, o_qi:o_ki], w_in[:, o_v:o_qi], w_in[:, o_wi:o_u]],
                         axis=1).T.astype(BF16)
    return wn, wt


def _layer_weights(w1_gate, w1_up, w1_down, w_in, w_out, w2_gate, w2_up, w2_down):
    wn, wt = _pack_in_weights(w_in)
    return dict(w1g=w1_gate.astype(BF16), w1u=w1_up.astype(BF16), w1d=w1_down.astype(BF16), wn=wn, wt=wt,
                wo=w_out.astype(BF16), w2g=w2_gate.astype(BF16), w2u=w2_up.astype(BF16), w2d=w2_down.astype(BF16))


def _alibi_slopes():
    return jnp.exp2(-8.0 * jnp.arange(1, N_HEADS + 1, dtype=F32) / N_HEADS)


def _forward(x_prompt, x_sample, cache_k, cache_v, cache_kidx, page_table, c_prompt, c_sample,
             w_ada, b_ada, norm_ffn1, w1_gate, w1_up, w1_down, norm_mix, w_in, q_norm, k_norm,
             idx_k_norm, w_spatial, b_spatial, v_norm, out_norm_a, out_norm_b, w_out, norm_ffn2,
             w2_gate, w2_up, w2_down):
    batch, seq, d = x_prompt.shape
    nseq, ntok, _ = x_sample.shape
    n_pages = page_table.shape[1]
    w_attn = N_HEADS * HEAD_DIM
    w_g = N_GROUPS * GROUP_DIM
    assert ntok == DEC_ROWS and seq % 256 == 0 and cache_k.shape[0] == 1
    tq = 256
    tm_ffn = 512 if (batch * seq) % 512 == 0 else 256
    m_s = nseq * ntok

    wts = _layer_weights(w1_gate[0], w1_up[0], w1_down[0], w_in[0], w_out[0], w2_gate[0], w2_up[0], w2_down[0])

    n_c = batch + nseq
    c_all = jnp.concatenate([c_prompt, c_sample, jnp.zeros((-n_c % 8, d), F32)], axis=0)
    mods = _ada_call(c_all, w_ada[0], b_ada[0])
    mods_p = [mods[:batch, i * d:(i + 1) * d].reshape(batch, 1, d) for i in range(N_MOD)]
    mods_s = [jnp.repeat(mods[batch:n_c, i * d:(i + 1) * d], ntok, axis=0).reshape(1, m_s, d) for i in range(N_MOD)]

    def stream(x, mod, tm_f, tm_p, tiles_f, tiles_p):
        sh1, sc1, g1, sh2, sc2, g2, sh3, sc3, g3 = mod
        h = _ffn_call(x, sh1, sc1, g1, norm_ffn1[0], wts["w1g"], wts["w1u"], wts["w1d"],
                      tm=tm_f, tiles_per_group=tiles_f)
        proj = _inproj_call(h, sh2, sc2, norm_mix[0], wts["wn"], wts["wt"], q_norm[0], k_norm[0],
                            idx_k_norm[0], v_norm[0], tm=tm_p, tiles_per_group=tiles_p)
        return h, proj, (g2, sh3, sc3, g3)

    def finish(h, out_a, u, vb, rest, ws, bs_t, tm_f, tm_p, tiles_f, tiles_p, unit):
        g2, sh3, sc3, g3 = rest
        h2 = _outproj_call(out_a, u, vb, h, g2, ws, bs_t, out_norm_a[0], out_norm_b[0], wts["wo"],
                           tm=tm_p, tiles_per_group=tiles_p, unit=unit)
        return _ffn_call(h2, sh3, sc3, g3, norm_ffn2[0], wts["w2g"], wts["w2u"], wts["w2d"],
                         tm=tm_f, tiles_per_group=tiles_f)

    xp = x_prompt.reshape(batch * seq, d)
    tf_p, tp_p = seq // tm_ffn, seq // tq
    h_p, proj_p, rest_p = stream(xp, mods_p, tm_ffn, tq, tf_p, tp_p)
    qT, qiT, wT, vT, k_p, kb_p, v_p, ki_p, kib_p, u_p, vb_p = proj_p
    out_a_p = _attn_prompt_call(qT, qiT, wT, kb_p, vT, kib_p, batch=batch, seq=seq, tq=tq)
    y_p = finish(h_p, out_a_p, u_p, vb_p, rest_p, w_spatial[0], b_spatial[0].T, tm_ffn, tq, tf_p, tp_p, CHUNK)

    xs = x_sample.reshape(m_s, d)
    h_s, proj_s, rest_s = stream(xs, mods_s, m_s, m_s, 1, 1)
    qT_s, qiT_s, wT_s, _, k_s, kb_s, v_s, ki_s, kib_s, u_s, vb_s = proj_s
    q_s = qT_s[0].T.reshape(nseq, ntok, N_HEADS, HEAD_DIM)
    head_eye = jnp.eye(N_HEADS, dtype=BF16)
    q_bd = jnp.einsum("bthd,hg->bhtgd", q_s, head_eye).reshape(nseq, N_HEADS * ntok, w_attn)
    qi2d = qiT_s[0].reshape(IDX_HEADS, IDX_DIM, nseq, ntok).transpose(2, 0, 3, 1).reshape(
        nseq, IDX_HEADS * ntok, IDX_DIM)
    wcol = wT_s[0].reshape(IDX_HEADS, nseq, ntok).transpose(1, 0, 2).reshape(nseq, IDX_HEADS * ntok, 1)
    pad_rows = PAGE_SIZE - ntok

    def pad_page(a):
        return jnp.pad(a.reshape(nseq, ntok, -1), ((0, 0), (0, pad_rows), (0, 0)))

    dm = _select_decode_call(page_table, qi2d, wcol, pad_page(kib_s),
                             cache_kidx[0], pages_per_step=math.gcd(n_pages, 16))
    slope_col = jnp.repeat(_alibi_slopes(), ntok).reshape(N_HEADS * ntok, 1)
    n_pool = cache_k.shape[1]
    out_a_s = _attn_decode_call(page_table, q_bd, slope_col, dm, pad_page(kb_s), pad_page(v_s.astype(BF16)),
                                cache_k[0].reshape(n_pool, PAGE_SIZE, w_attn),
                                cache_v[0].reshape(n_pool, PAGE_SIZE, w_attn),
                                pages_per_step=math.gcd(n_pages, 4))
    ws_s = jnp.einsum("ab,gts->gatbs", jnp.eye(nseq, dtype=F32), w_spatial[0][:, :ntok, :ntok]).reshape(
        N_GROUPS, m_s, m_s)
    bs_s = jnp.tile(b_spatial[0][:, :ntok].T, (nseq, 1))
    y_s = finish(h_s, out_a_s.reshape(m_s, w_attn), u_s, vb_s, rest_s, ws_s, bs_s, m_s, m_s, 1, 1, m_s)

    return (
        y_p.reshape(batch, seq, d),
        y_s.reshape(nseq, ntok, d),
        k_p.reshape(1, batch, seq, N_HEADS, HEAD_DIM),
        v_p.reshape(1, batch, seq, N_HEADS, HEAD_DIM),
        ki_p.reshape(1, batch, seq, IDX_DIM),
        k_s.reshape(1, nseq, ntok, N_HEADS, HEAD_DIM),
        v_s.reshape(1, nseq, ntok, N_HEADS, HEAD_DIM),
        ki_s.reshape(1, nseq, ntok, IDX_DIM),
        vb_s.reshape(1, nseq, ntok, w_g),
    )


def kernel(x_prompt, x_sample, cache_k, cache_v, cache_kidx, page_table, c_prompt, c_sample, w_ada, b_ada, norm_ffn1, w1_gate, w1_up, w1_down, norm_mix, w_in, q_norm, k_norm, idx_k_norm, w_spatial, b_spatial, v_norm, out_norm_a, out_norm_b, w_out, norm_ffn2, w2_gate, w2_up, w2_down):
    return _forward(x_prompt, x_sample, cache_k, cache_v, cache_kidx, page_table, c_prompt, c_sample,
                    w_ada, b_ada, norm_ffn1, w1_gate, w1_up, w1_down, norm_mix, w_in, q_norm, k_norm,
                    idx_k_norm, w_spatial, b_spatial, v_norm, out_norm_a, out_norm_b, w_out, norm_ffn2,
                    w2_gate, w2_up, w2_down)
```

```python
import functools
import math

import jax
import jax.numpy as jnp
from jax import lax
from jax.experimental import pallas as pl
from jax.experimental.pallas import tpu as pltpu

F32 = jnp.float32
BF16 = jnp.bfloat16
I32 = jnp.int32

EPS = 1e-6
HEAD_DIM = 128
N_HEADS = 8
N_GROUPS = 8
GROUP_DIM = 128
CHUNK = 128
IDX_HEADS = 16
IDX_DIM = 64
TOPK_MAX = 256
PAGE_SIZE = 128
N_MOD = 9
SUBLANES = 8
PAGE_ROWS = PAGE_SIZE * N_HEADS

INT_MIN = -(2 ** 31)
INT_MAX = 2 ** 31 - 1
F32_MIN = float(jnp.finfo(jnp.float32).min)
LOG2E = math.log2(math.e)

VMEM_LIMIT_BYTES = 60000 * 1024

NT_DIMS = (((1,), (1,)), ((), ()))


def _compiler_params(semantics):
    return pltpu.CompilerParams(dimension_semantics=semantics, vmem_limit_bytes=VMEM_LIMIT_BYTES)


def _modulate(x, gain, shift, scale):
    ms = jnp.mean(x * x, axis=-1, keepdims=True)
    y = x * lax.rsqrt(ms + EPS)
    return (y * gain) * (1.0 + scale) + shift


def _rms_rows(x, gain):
    ms = jnp.mean(x * x, axis=-1, keepdims=True)
    return (x * lax.rsqrt(ms + EPS)) * gain


def _div_pow2(x, n):
    assert n & (n - 1) == 0
    return lax.shift_right_logical(x, jnp.int32(n.bit_length() - 1))


def _mod_pow2(x, n):
    assert n & (n - 1) == 0
    return x & jnp.int32(n - 1)


LANES = 128


def _fold_lane_tiles(x, op):
    return functools.reduce(op, [x[:, i * LANES:(i + 1) * LANES] for i in range(x.shape[1] // LANES)])


def _rms_cols(x, gain_col):
    ms = jnp.mean(x * x, axis=0, keepdims=True)
    return (x * lax.rsqrt(ms + EPS)) * gain_col


def _ada_kernel(c_ref, w_ref, b_ref, o_ref):
    a = jax.nn.silu(c_ref[...]).astype(BF16)
    o_ref[...] = jnp.dot(a, w_ref[...].astype(BF16), preferred_element_type=F32) + b_ref[...]


def _ada_call(c_all, w_ada, b_ada):
    rows, d = c_all.shape
    n = w_ada.shape[1]
    tn = 1024
    return pl.pallas_call(
        _ada_kernel,
        grid=(n // tn,),
        in_specs=[
            pl.BlockSpec((rows, d), lambda j: (0, 0)),
            pl.BlockSpec((d, tn), lambda j: (0, j)),
            pl.BlockSpec((1, tn), lambda j: (0, j)),
        ],
        out_specs=pl.BlockSpec((rows, tn), lambda j: (0, j)),
        out_shape=jax.ShapeDtypeStruct((rows, n), F32),
        compiler_params=_compiler_params(("parallel",)),
        name="ada_proj",
    )(c_all, w_ada, b_ada.reshape(1, n))


def _ffn_kernel(x_ref, sh_ref, sc_ref, g_ref, nw_ref, wg_ref, wu_ref, wd_ref, o_ref, xn_ref, acc_ref):
    n = pl.program_id(1)

    @pl.when(n == 0)
    def _():
        xn_ref[...] = _modulate(x_ref[...], nw_ref[...], sh_ref[...], sc_ref[...]).astype(BF16)
        acc_ref[...] = jnp.zeros_like(acc_ref)

    xn = xn_ref[...]
    gate = jnp.dot(xn, wg_ref[...], preferred_element_type=F32)
    up = jnp.dot(xn, wu_ref[...], preferred_element_type=F32)
    hmid = (jax.nn.silu(gate) * up).astype(BF16)
    acc_ref[...] += jnp.dot(hmid, wd_ref[...], preferred_element_type=F32)

    @pl.when(n == pl.num_programs(1) - 1)
    def _():
        o_ref[...] = x_ref[...] + (0.5 * g_ref[...]) * acc_ref[...]


def _mod_spec(mod, tiles_per_group):
    _, r, d = mod.shape
    return pl.BlockSpec((None, r, d), lambda i, *_: (i // tiles_per_group, 0, 0))


def _ffn_call(x, shift, scale, gate, norm_w, wg, wu, wd, *, tm, tiles_per_group):
    m, d = x.shape
    dff = wg.shape[1]
    tn = 512
    return pl.pallas_call(
        _ffn_kernel,
        grid=(m // tm, dff // tn),
        in_specs=[
            pl.BlockSpec((tm, d), lambda i, n: (i, 0)),
            _mod_spec(shift, tiles_per_group),
            _mod_spec(scale, tiles_per_group),
            _mod_spec(gate, tiles_per_group),
            pl.BlockSpec((1, d), lambda i, n: (0, 0)),
            pl.BlockSpec((d, tn), lambda i, n: (0, n)),
            pl.BlockSpec((d, tn), lambda i, n: (0, n)),
            pl.BlockSpec((tn, d), lambda i, n: (n, 0)),
        ],
        out_specs=pl.BlockSpec((tm, d), lambda i, n: (i, 0)),
        out_shape=jax.ShapeDtypeStruct((m, d), F32),
        scratch_shapes=[pltpu.VMEM((tm, d), BF16), pltpu.VMEM((tm, d), F32)],
        compiler_params=_compiler_params(("parallel", "arbitrary")),
        name="ffn",
    )(x, shift, scale, gate, norm_w.reshape(1, d), wg, wu, wd)


def _inproj_kernel(h_ref, sh_ref, sc_ref, nw_ref, wn_ref, wt_ref, qn_ref, kn_ref, ikn_ref, vn_ref,
                   qT_ref, qiT_ref, wT_ref, vT_ref, k_ref, kb_ref, v_ref, ki_ref, kib_ref, u_ref, vb_ref):
    w_attn = N_HEADS * HEAD_DIM
    w_idx = IDX_HEADS * IDX_DIM
    a = _modulate(h_ref[...], nw_ref[...], sh_ref[...], sc_ref[...]).astype(BF16)

    q_t = lax.dot_general(wt_ref[0:w_attn, :], a, NT_DIMS, preferred_element_type=F32)
    for h in range(N_HEADS):
        rows = slice(h * HEAD_DIM, (h + 1) * HEAD_DIM)
        qT_ref[rows, :] = _rms_cols(q_t[rows, :], qn_ref[...]).astype(BF16)
    qi_t = lax.dot_general(wt_ref[w_attn:w_attn + w_idx, :], a, NT_DIMS, preferred_element_type=F32)
    qiT_ref[...] = (qi_t * (IDX_DIM ** -0.5)).astype(BF16)
    v_t = lax.dot_general(wt_ref[w_attn + w_idx:2 * w_attn + w_idx, :], a, NT_DIMS, preferred_element_type=F32)
    vT_ref[...] = v_t.astype(BF16)
    w_t = lax.dot_general(wt_ref[2 * w_attn + w_idx:2 * w_attn + w_idx + IDX_HEADS, :], a, NT_DIMS,
                          preferred_element_type=F32)
    wT_ref[...] = w_t * (IDX_HEADS ** -0.5)

    k = jnp.dot(a, wn_ref[:, 0:w_attn], preferred_element_type=F32)
    for h in range(N_HEADS):
        cols = slice(h * HEAD_DIM, (h + 1) * HEAD_DIM)
        kh = _rms_rows(k[:, cols], kn_ref[...])
        k_ref[:, cols] = kh
        kb_ref[:, cols] = kh.astype(BF16)
    v_ref[...] = jnp.dot(a, wn_ref[:, w_attn:2 * w_attn], preferred_element_type=F32)
    u_ref[...] = jax.nn.gelu(jnp.dot(a, wn_ref[:, 2 * w_attn:3 * w_attn], preferred_element_type=F32))
    vb = jax.nn.gelu(jnp.dot(a, wn_ref[:, 3 * w_attn:4 * w_attn], preferred_element_type=F32))
    for g in range(N_GROUPS):
        cols = slice(g * GROUP_DIM, (g + 1) * GROUP_DIM)
        vb_ref[:, cols] = _rms_rows(vb[:, cols], vn_ref[g:g + 1, :])
    kiw = jnp.dot(a, wn_ref[:, 4 * w_attn:4 * w_attn + 128], preferred_element_type=F32)
    ki = _rms_rows(kiw[:, 0:IDX_DIM], ikn_ref[...])
    ki_ref[...] = ki
    kib_ref[...] = ki.astype(BF16)


def _inproj_call(h, shift, scale, norm_w, wn, wt, q_norm, k_norm, idx_k_norm, v_norm, *, tm, tiles_per_group):
    m, d = h.shape
    nt = m // tm
    w_attn = N_HEADS * HEAD_DIM
    w_idx = IDX_HEADS * IDX_DIM
    w_g = N_GROUPS * GROUP_DIM
    row = lambda i: (i, 0)
    fm = lambda i: (i, 0, 0)
    const = lambda i: (0, 0)
    out_shapes = [
        jax.ShapeDtypeStruct((nt, w_attn, tm), BF16),
        jax.ShapeDtypeStruct((nt, w_idx, tm), BF16),
        jax.ShapeDtypeStruct((nt, IDX_HEADS, tm), F32),
        jax.ShapeDtypeStruct((nt, w_attn, tm), BF16),
        jax.ShapeDtypeStruct((m, w_attn), F32),
        jax.ShapeDtypeStruct((m, w_attn), BF16),
        jax.ShapeDtypeStruct((m, w_attn), F32),
        jax.ShapeDtypeStruct((m, IDX_DIM), F32),
        jax.ShapeDtypeStruct((m, IDX_DIM), BF16),
        jax.ShapeDtypeStruct((m, w_g), F32),
        jax.ShapeDtypeStruct((m, w_g), F32),
    ]
    out_specs = [
        pl.BlockSpec((None, w_attn, tm), fm),
        pl.BlockSpec((None, w_idx, tm), fm),
        pl.BlockSpec((None, IDX_HEADS, tm), fm),
        pl.BlockSpec((None, w_attn, tm), fm),
        pl.BlockSpec((tm, w_attn), row),
        pl.BlockSpec((tm, w_attn), row),
        pl.BlockSpec((tm, w_attn), row),
        pl.BlockSpec((tm, IDX_DIM), row),
        pl.BlockSpec((tm, IDX_DIM), row),
        pl.BlockSpec((tm, w_g), row),
        pl.BlockSpec((tm, w_g), row),
    ]
    return pl.pallas_call(
        _inproj_kernel,
        grid=(nt,),
        in_specs=[
            pl.BlockSpec((tm, d), row),
            _mod_spec(shift, tiles_per_group),
            _mod_spec(scale, tiles_per_group),
            pl.BlockSpec((1, d), const),
            pl.BlockSpec(wn.shape, const),
            pl.BlockSpec(wt.shape, const),
            pl.BlockSpec((HEAD_DIM, 1), const),
            pl.BlockSpec((1, HEAD_DIM), const),
            pl.BlockSpec((1, IDX_DIM), const),
            pl.BlockSpec((N_GROUPS, GROUP_DIM), const),
        ],
        out_specs=out_specs,
        out_shape=out_shapes,
        compiler_params=_compiler_params(("parallel",)),
        name="in_proj",
    )(h, shift, scale, norm_w.reshape(1, d), wn, wt, q_norm.reshape(HEAD_DIM, 1), k_norm.reshape(1, HEAD_DIM),
      idx_k_norm.reshape(1, IDX_DIM), v_norm)


def _key_to_float(key):
    return lax.bitcast_convert_type(jnp.where(key < 0, key ^ jnp.int32(INT_MAX), key), F32)


def _topk_threshold(count_ge, topk, like):
    zero = jnp.zeros_like(like)
    cnt0 = count_ge(_key_to_float(zero))
    nonneg = cnt0 >= topk
    t0 = jnp.where(nonneg, zero, jnp.int32(INT_MIN))
    c0 = jnp.where(nonneg, cnt0, jnp.int32(topk))

    def body(p, carry):
        t, ct = carry
        cand = t | jnp.left_shift(jnp.int32(1), 31 - p)
        cnt = count_ge(_key_to_float(cand))
        take = cnt >= topk
        return jnp.where(take, cand, t), jnp.where(take, cnt, ct)

    t, ct = lax.fori_loop(1, 32, body, (t0, c0))
    return t != jnp.int32(INT_MIN), _key_to_float(t), ct


def _tie_cutoff(count_where, thr, topk, nbits, excess):
    need = topk - count_where(lambda sc, pos: sc > thr)
    j = jnp.zeros_like(need)
    for bit in reversed(range(nbits)):
        cand = j | jnp.int32(1 << bit)
        cnt = count_where(lambda sc, pos, cand=cand: (sc == thr) & (pos < cand))
        j = jnp.where(cnt < need, cand, j)
    return jnp.where(excess, j, jnp.int32(INT_MAX))


def _select(count_where, topk, pos_bits, like):
    has_thr, thr, cnt_thr = _topk_threshold(lambda t: count_where(lambda sc, pos: sc >= t), topk, like)
    excess = (cnt_thr > topk) & has_thr
    cutoff = lax.cond(
        jnp.max(excess.astype(I32)) > 0,
        lambda: _tie_cutoff(count_where, thr, topk, pos_bits, excess),
        lambda: jnp.full(like.shape, INT_MAX, I32),
    )
    return lambda sc, pos: jnp.logical_not(has_thr) | (sc > thr) | ((sc == thr) & (pos <= cutoff))


def _attn_prompt_kernel(qT_ref, qiT_ref, wT_ref, kb_ref, vT_ref, kib_ref, o_ref,
                        sc_ref, dm_ref, acc_ref, m_ref, l_ref, p_ref, *, tq, topk, pos_bits):
    j = pl.program_id(1)
    nchunks = j + 1
    groups = tq // SUBLANES
    t_pos = j * tq + lax.broadcasted_iota(I32, (1, tq), 1)
    s_off = lax.broadcasted_iota(I32, (tq, 1), 0)

    def chunk_start(c):
        return pl.multiple_of(c * tq, tq)

    def score_body(c, carry):
        off = chunk_start(c)
        kic = kib_ref[pl.ds(off, tq), :]
        sc = jnp.zeros((tq, tq), F32)
        for h in range(IDX_HEADS):
            d = jnp.dot(kic, qiT_ref[h * IDX_DIM:(h + 1) * IDX_DIM, :], preferred_element_type=F32)
            sc = sc + jnp.maximum(d, 0.0) * wT_ref[h:h + 1, :]
        sc_ref[pl.ds(off, tq), :] = jnp.where((off + s_off) <= t_pos, sc, -jnp.inf)
        return carry

    lax.fori_loop(0, nchunks, score_body, 0)

    def count_where(pred):
        def body(c, cnt):
            off = chunk_start(c)
            hit = pred(sc_ref[pl.ds(off, tq), :], off + s_off).astype(I32)
            return cnt + jnp.sum(hit.reshape(groups, SUBLANES, tq), axis=0)

        cnt8 = lax.fori_loop(0, nchunks, body, jnp.zeros((SUBLANES, tq), I32))
        return jnp.sum(cnt8, axis=0, keepdims=True)

    selected = _select(count_where, topk, pos_bits, t_pos)

    def dm_body(c, carry):
        off = chunk_start(c)
        pos = off + s_off
        sel = selected(sc_ref[pl.ds(off, tq), :], pos) & (pos <= t_pos)
        dm_ref[pl.ds(off, tq), :] = jnp.where(sel, (t_pos - pos).astype(F32) * LOG2E, jnp.inf)
        return carry

    lax.fori_loop(0, nchunks, dm_body, 0)

    m_ref[...] = jnp.full(m_ref.shape, F32_MIN, F32)
    l_ref[...] = jnp.zeros_like(l_ref)
    acc_ref[...] = jnp.zeros_like(acc_ref)
    qk_scale = (HEAD_DIM ** -0.5) * LOG2E

    def logits(off, h, dmc):
        rows = slice(h * HEAD_DIM, (h + 1) * HEAD_DIM)
        qk = jnp.dot(kb_ref[pl.ds(off, tq), rows], qT_ref[rows, :], preferred_element_type=F32)
        return (qk * qk_scale - (2.0 ** -(h + 1)) * dmc).reshape(groups, SUBLANES, tq)

    def max_body(c, carry):
        off = chunk_start(c)
        dmc = dm_ref[pl.ds(off, tq), :]
        for h in range(N_HEADS):
            m_ref[h] = jnp.maximum(m_ref[h], jnp.max(logits(off, h, dmc), axis=0))
        return carry

    lax.fori_loop(0, nchunks, max_body, 0)
    for h in range(N_HEADS):
        m_ref[h] = jnp.broadcast_to(jnp.max(m_ref[h], axis=0, keepdims=True), (SUBLANES, tq))

    def att_body(c, carry):
        off = chunk_start(c)
        dmc = dm_ref[pl.ds(off, tq), :]
        for h in range(N_HEADS):
            p = jnp.exp2(logits(off, h, dmc) - m_ref[h][None])
            l_ref[h] += jnp.sum(p, axis=0)
            p_ref[h] = p.reshape(tq, tq).astype(BF16)
        for h in range(N_HEADS):
            rows = slice(h * HEAD_DIM, (h + 1) * HEAD_DIM)
            acc_ref[rows, :] += jnp.dot(vT_ref[c, rows, :], p_ref[h], preferred_element_type=F32)
        return carry

    lax.fori_loop(0, nchunks, att_body, 0)

    for h in range(N_HEADS):
        rows = slice(h * HEAD_DIM, (h + 1) * HEAD_DIM)
        acc_ref[rows, :] = acc_ref[rows, :] / jnp.sum(l_ref[h], axis=0, keepdims=True)
    o_ref[...] = acc_ref[...].T


def _attn_prompt_call(qT, qiT, wT, kb, vT, kib, *, batch, seq, tq):
    w_attn = N_HEADS * HEAD_DIM
    w_idx = IDX_HEADS * IDX_DIM
    nq = seq // tq
    topk = min(TOPK_MAX, seq // 4)
    qblk = lambda b, j: (b * nq + j, 0, 0)
    kernel = functools.partial(_attn_prompt_kernel, tq=tq, topk=topk, pos_bits=max(1, (seq - 1).bit_length()))
    return pl.pallas_call(
        kernel,
        grid=(batch, nq),
        in_specs=[
            pl.BlockSpec((None, w_attn, tq), qblk),
            pl.BlockSpec((None, w_idx, tq), qblk),
            pl.BlockSpec((None, IDX_HEADS, tq), qblk),
            pl.BlockSpec((seq, w_attn), lambda b, j: (b, 0), pipeline_mode=pl.Buffered(1)),
            pl.BlockSpec((nq, w_attn, tq), lambda b, j: (b, 0, 0), pipeline_mode=pl.Buffered(1)),
            pl.BlockSpec((seq, IDX_DIM), lambda b, j: (b, 0), pipeline_mode=pl.Buffered(1)),
        ],
        out_specs=pl.BlockSpec((tq, w_attn), lambda b, j: (b * nq + j, 0)),
        out_shape=jax.ShapeDtypeStruct((batch * seq, w_attn), F32),
        scratch_shapes=[
            pltpu.VMEM((seq, tq), F32),
            pltpu.VMEM((seq, tq), F32),
            pltpu.VMEM((w_attn, tq), F32),
            pltpu.VMEM((N_HEADS, SUBLANES, tq), F32),
            pltpu.VMEM((N_HEADS, SUBLANES, tq), F32),
            pltpu.VMEM((N_HEADS, tq, tq), BF16),
        ],
        compiler_params=_compiler_params(("parallel", "arbitrary")),
        name="attn_prompt",
    )(qT, qiT, wT, kb, vT, kib)


def _outproj_kernel(oa_ref, u_ref, vb_ref, h_ref, g_ref, ws_ref, bs_ref, na_ref, nb_ref, wo_ref, o_ref, *, unit):
    tm = oa_ref.shape[0]
    w_attn = N_HEADS * HEAD_DIM
    rows_i = lax.broadcasted_iota(I32, (unit, unit), 0)
    cols_i = lax.broadcasted_iota(I32, (unit, unit), 1)
    causal = rows_i >= cols_i
    ob_parts = []
    for g in range(N_GROUPS):
        cols = slice(g * GROUP_DIM, (g + 1) * GROUP_DIM)
        wm = jnp.where(causal, ws_ref[g], 0.0).astype(BF16)
        bias = bs_ref[:, g:g + 1]
        mixed = []
        for r in range(tm // unit):
            rws = slice(r * unit, (r + 1) * unit)
            mixed.append(jnp.dot(wm, vb_ref[rws, cols].astype(BF16), preferred_element_type=F32) + bias)
        mixed = mixed[0] if len(mixed) == 1 else jnp.concatenate(mixed, axis=0)
        ob_parts.append(u_ref[:, cols] * mixed)
    ob = jnp.concatenate(ob_parts, axis=1)
    za = _rms_rows(oa_ref[...], na_ref[...]).astype(BF16)
    zb = _rms_rows(ob, nb_ref[...]).astype(BF16)
    mix = (jnp.dot(za, wo_ref[0:w_attn, :], preferred_element_type=F32)
           + jnp.dot(zb, wo_ref[w_attn:, :], preferred_element_type=F32))
    o_ref[...] = h_ref[...] + g_ref[...] * mix


def _outproj_call(out_a, u, vb, h, gate, ws, bs_t, norm_a, norm_b, w_out, *, tm, tiles_per_group, unit):
    m, d = h.shape
    w_attn = out_a.shape[1]
    w_g = u.shape[1]
    row = lambda i: (i, 0)
    const = lambda i: (0, 0)
    return pl.pallas_call(
        functools.partial(_outproj_kernel, unit=unit),
        grid=(m // tm,),
        in_specs=[
            pl.BlockSpec((tm, w_attn), row),
            pl.BlockSpec((tm, w_g), row),
            pl.BlockSpec((tm, w_g), row),
            pl.BlockSpec((tm, d), row),
            _mod_spec(gate, tiles_per_group),
            pl.BlockSpec(ws.shape, lambda i: (0, 0, 0)),
            pl.BlockSpec(bs_t.shape, const),
            pl.BlockSpec((1, w_attn), const),
            pl.BlockSpec((1, w_g), const),
            pl.BlockSpec(w_out.shape, const),
        ],
        out_specs=pl.BlockSpec((tm, d), row),
        out_shape=jax.ShapeDtypeStruct((m, d), F32),
        compiler_params=_compiler_params(("parallel",)),
        name="out_proj",
    )(out_a, u, vb, h, gate, ws, bs_t, norm_a.reshape(1, w_attn), norm_b.reshape(1, w_g), w_out)


def _select_decode_kernel(pt_ref, qi_ref, w_ref, kin_ref, *rest, pages_per_step, n_pages, topk, pos_bits):
    page_refs = rest[:pages_per_step]
    bias_ref = rest[pages_per_step]
    sc_ref = rest[pages_per_step + 1]
    p = pl.program_id(1)
    qi = qi_ref[...]
    w = w_ref[...]

    def page_scores(keys_bf16):
        d = lax.dot_general(qi, keys_bf16, NT_DIMS, preferred_element_type=F32)
        r = jnp.maximum(d, 0.0) * w
        return jnp.sum(r.reshape(IDX_HEADS, SUBLANES, PAGE_SIZE), axis=0)

    for i in range(pages_per_step):
        sc_ref[p * pages_per_step + i] = page_scores(page_refs[i][...].astype(BF16))

    @pl.when(p == pl.num_programs(1) - 1)
    def _():
        shape = (n_pages + 1, SUBLANES, PAGE_SIZE)
        tok = lax.broadcasted_iota(I32, (SUBLANES, PAGE_SIZE), 0)
        lane = lax.broadcasted_iota(I32, (SUBLANES, PAGE_SIZE), 1)
        sc_ref[n_pages] = jnp.where(lane <= tok, page_scores(kin_ref[...]), -jnp.inf)
        pos = lax.broadcasted_iota(I32, shape, 0) * PAGE_SIZE + lax.broadcasted_iota(I32, shape, 2)
        q_pos = n_pages * PAGE_SIZE + lax.broadcasted_iota(I32, shape, 1)

        def count_where(pred):
            hit = pred(sc_ref[...], pos).astype(I32)
            return jnp.sum(jnp.sum(hit, axis=0), axis=-1, keepdims=True)

        selected = _select(count_where, topk, pos_bits, jnp.zeros((SUBLANES, 1), I32))
        sel = selected(sc_ref[...], pos) & (pos <= q_pos)
        sel2d = jnp.where(sel, 1.0, 0.0).astype(BF16).reshape((n_pages + 1) * SUBLANES, PAGE_SIZE)
        expand = (_div_pow2(lax.broadcasted_iota(I32, (PAGE_SIZE, PAGE_ROWS), 1), N_HEADS)
                  == lax.broadcasted_iota(I32, (PAGE_SIZE, PAGE_ROWS), 0))
        rep = jnp.dot(sel2d, jnp.where(expand, 1.0, 0.0).astype(BF16), preferred_element_type=F32)
        bias_ref[...] = jnp.where(rep > 0.5, 0.0, -jnp.inf).reshape(n_pages + 1, SUBLANES, PAGE_ROWS)


def _select_decode_call(page_table, qi2d, wcol, ki_new, cache_kidx, *, pages_per_step):
    nseq, n_pages = page_table.shape
    topk = min(TOPK_MAX, (n_pages * PAGE_SIZE + SUBLANES) // 4)
    pos_bits = max(1, ((n_pages + 1) * PAGE_SIZE - 1).bit_length())
    per_seq = lambda b, p, pt: (b, 0, 0)

    def page_spec(i):
        return pl.BlockSpec((None, PAGE_SIZE, IDX_DIM),
                            lambda b, p, pt, i=i: (pt[b, p * pages_per_step + i], 0, 0))

    kernel = functools.partial(_select_decode_kernel, pages_per_step=pages_per_step, n_pages=n_pages,
                               topk=topk, pos_bits=pos_bits)
    grid_spec = pltpu.PrefetchScalarGridSpec(
        num_scalar_prefetch=1,
        grid=(nseq, n_pages // pages_per_step),
        in_specs=[
            pl.BlockSpec((None, IDX_HEADS * SUBLANES, IDX_DIM), per_seq),
            pl.BlockSpec((None, IDX_HEADS * SUBLANES, 1), per_seq),
            pl.BlockSpec((None, PAGE_SIZE, IDX_DIM), per_seq),
        ] + [page_spec(i) for i in range(pages_per_step)],
        out_specs=pl.BlockSpec((None, n_pages + 1, SUBLANES, PAGE_ROWS), lambda b, p, pt: (b, 0, 0, 0)),
        scratch_shapes=[pltpu.VMEM((n_pages + 1, SUBLANES, PAGE_SIZE), F32)],
    )
    return pl.pallas_call(
        kernel,
        grid_spec=grid_spec,
        out_shape=jax.ShapeDtypeStruct((nseq, n_pages + 1, SUBLANES, PAGE_ROWS), F32),
        compiler_params=_compiler_params(("parallel", "arbitrary")),
        name="select_decode",
    )(page_table, qi2d, wcol, ki_new, *([cache_kidx] * pages_per_step))


def _attn_decode_kernel(pt_ref, q_ref, slope_ref, bias_ref, kn_ref, vn_ref, *rest, pages_per_step, n_pages):
    k_refs = rest[:pages_per_step]
    v_refs = rest[pages_per_step:2 * pages_per_step]
    o_ref, a_ref, acc_ref, m_ref, l_ref = rest[2 * pages_per_step:]
    p = pl.program_id(1)
    nrow = N_HEADS * SUBLANES
    qk_scale = (HEAD_DIM ** -0.5) * LOG2E
    slope = slope_ref[...] * LOG2E

    @pl.when(p == 0)
    def _():
        m_ref[...] = jnp.full(m_ref.shape, F32_MIN, F32)
        l_ref[...] = jnp.zeros_like(l_ref)
        acc_ref[...] = jnp.zeros_like(acc_ref)
        row = lax.broadcasted_iota(I32, (nrow, PAGE_ROWS), 0)
        col = lax.broadcasted_iota(I32, (nrow, PAGE_ROWS), 1)
        same_head = _mod_pow2(col, N_HEADS) == _div_pow2(row, SUBLANES)
        tok_minus_key = (_mod_pow2(row, SUBLANES) - _div_pow2(col, N_HEADS)).astype(F32)
        a_ref[...] = jnp.where(same_head, -slope * tok_minus_key, -jnp.inf)

    def logits(keys_bf16, page):
        page_dist = jnp.full((nrow, 1), (n_pages - page) * PAGE_SIZE, I32).astype(F32)
        s = lax.dot_general(q_ref[...], keys_bf16, NT_DIMS, preferred_element_type=F32)
        return s * qk_scale + a_ref[...] - slope * page_dist + jnp.concatenate([bias_ref[page]] * N_HEADS, axis=0)

    def update(s_pages, v_pages):
        lanes_max = functools.reduce(jnp.maximum, [_fold_lane_tiles(s, jnp.maximum) for s in s_pages])
        m_old = m_ref[...]
        m_new = jnp.maximum(m_old, jnp.max(lanes_max, axis=-1, keepdims=True))
        alpha = jnp.exp2(m_old - m_new)
        lanes_sum = jnp.zeros((nrow, PAGE_SIZE), F32)
        acc = alpha * acc_ref[...]
        for s, vals in zip(s_pages, v_pages):
            pr = jnp.exp2(s - m_new)
            lanes_sum = lanes_sum + _fold_lane_tiles(pr, jnp.add)
            acc = acc + jnp.dot(pr.astype(BF16), vals, preferred_element_type=F32)
        l_ref[...] = alpha * l_ref[...] + jnp.sum(lanes_sum, axis=-1, keepdims=True)
        acc_ref[...] = acc
        m_ref[...] = m_new

    update([logits(k_refs[i][...].astype(BF16), p * pages_per_step + i) for i in range(pages_per_step)],
           [v_refs[i][...].astype(BF16) for i in range(pages_per_step)])

    @pl.when(p == pl.num_programs(1) - 1)
    def _():
        update([logits(kn_ref[...], n_pages)], [vn_ref[...]])
        o_ref[...] = acc_ref[...] / l_ref[...]


def _attn_decode_call(page_table, q_rows, slope_col, bias, k_new, v_new, cache_k, cache_v, *, pages_per_step):
    nseq, n_pages = page_table.shape
    nrow = N_HEADS * SUBLANES
    per_seq = lambda b, p, pt: (b, 0, 0)

    def page_spec(i):
        return pl.BlockSpec((None, PAGE_ROWS, HEAD_DIM),
                            lambda b, p, pt, i=i: (pt[b, p * pages_per_step + i], 0, 0))

    kernel = functools.partial(_attn_decode_kernel, pages_per_step=pages_per_step, n_pages=n_pages)
    grid_spec = pltpu.PrefetchScalarGridSpec(
        num_scalar_prefetch=1,
        grid=(nseq, n_pages // pages_per_step),
        in_specs=[
            pl.BlockSpec((None, nrow, HEAD_DIM), per_seq),
            pl.BlockSpec((nrow, 1), lambda b, p, pt: (0, 0)),
            pl.BlockSpec((None, n_pages + 1, SUBLANES, PAGE_ROWS), lambda b, p, pt: (b, 0, 0, 0)),
            pl.BlockSpec((None, PAGE_ROWS, HEAD_DIM), per_seq),
            pl.BlockSpec((None, PAGE_ROWS, HEAD_DIM), per_seq),
        ] + [page_spec(i) for i in range(pages_per_step)] + [page_spec(i) for i in range(pages_per_step)],
        out_specs=pl.BlockSpec((None, nrow, HEAD_DIM), per_seq),
        scratch_shapes=[
            pltpu.VMEM((nrow, PAGE_ROWS), F32),
            pltpu.VMEM((nrow, HEAD_DIM), F32),
            pltpu.VMEM((nrow, 1), F32),
            pltpu.VMEM((nrow, 1), F32),
        ],
    )
    return pl.pallas_call(
        kernel,
        grid_spec=grid_spec,
        out_shape=jax.ShapeDtypeStruct((nseq, nrow, HEAD_DIM), F32),
        compiler_params=_compiler_params(("parallel", "arbitrary")),
        name="attn_decode",
    )(page_table, q_rows, slope_col, bias, k_new, v_new,
      *([cache_k] * pages_per_step), *([cache_v] * pages_per_step))


def _pack_in_weights(w_in):
    w_attn = N_HEADS * HEAD_DIM
    w_idx = IDX_HEADS * IDX_DIM
    w_g = N_GROUPS * GROUP_DIM
    o_q, o_k, o_v = 0, w_attn, 2 * w_attn
    o_qi = 3 * w_attn
    o_ki = o_qi + w_idx
    o_wi = o_ki + IDX_DIM
    o_u = o_wi + IDX_HEADS
    o_vb = o_u + w_g
    d = w_in.shape[0]
    pad = jnp.zeros((d, 128 - IDX_DIM), w_in.dtype)
    wn = jnp.concatenate([w_in[:, o_k:o_v], w_in[:, o_v:o_qi], w_in[:, o_u:o_vb], w_in[:, o_vb:o_vb + w_g],
                          w_in[:, o_ki:o_wi], pad], axis=1).astype(BF16)
    wt = jnp.concatenate([w_in[:, o_q:o_k], w_in[:, o_qi:o_ki], w_in[:, o_v:o_qi], w_in[:, o_wi:o_u]],
                         axis=1).T.astype(BF16)
    return wn, wt


def _layer_weights(w1_gate, w1_up, w1_down, w_in, w_out, w2_gate, w2_up, w2_down):
    wn, wt = _pack_in_weights(w_in)
    return dict(w1g=w1_gate.astype(BF16), w1u=w1_up.astype(BF16), w1d=w1_down.astype(BF16), wn=wn, wt=wt,
                wo=w_out.astype(BF16), w2g=w2_gate.astype(BF16), w2u=w2_up.astype(BF16), w2d=w2_down.astype(BF16))


def _alibi_slopes():
    return jnp.exp2(-8.0 * jnp.arange(1, N_HEADS + 1, dtype=F32) / N_HEADS)


def _forward(x_prompt, x_sample, cache_k, cache_v, cache_kidx, page_table, c_prompt, c_sample,
             w_ada, b_ada, norm_ffn1, w1_gate, w1_up, w1_down, norm_mix, w_in, q_norm, k_norm,
             idx_k_norm, w_spatial, b_spatial, v_norm, out_norm_a, out_norm_b, w_out, norm_ffn2,
             w2_gate, w2_up, w2_down):
    batch, seq, d = x_prompt.shape
    nseq, ntok, _ = x_sample.shape
    n_pages = page_table.shape[1]
    n_pool = cache_k.shape[1]
    w_attn = N_HEADS * HEAD_DIM
    w_g = N_GROUPS * GROUP_DIM
    assert ntok == SUBLANES and seq % 256 == 0 and cache_k.shape[0] == 1
    tq = 256
    tm_ffn = 512 if (batch * seq) % 512 == 0 else 256
    m_s = nseq * ntok

    wts = _layer_weights(w1_gate[0], w1_up[0], w1_down[0], w_in[0], w_out[0], w2_gate[0], w2_up[0], w2_down[0])

    n_c = batch + nseq
    c_all = jnp.concatenate([c_prompt, c_sample, jnp.zeros((-n_c % 8, d), F32)], axis=0)
    mods = _ada_call(c_all, w_ada[0], b_ada[0])
    mods_p = [mods[:batch, i * d:(i + 1) * d].reshape(batch, 1, d) for i in range(N_MOD)]
    mods_s = [jnp.repeat(mods[batch:n_c, i * d:(i + 1) * d], ntok, axis=0).reshape(1, m_s, d) for i in range(N_MOD)]

    def stream(x, mod, tm_f, tm_p, tiles_f, tiles_p):
        sh1, sc1, g1, sh2, sc2, g2, sh3, sc3, g3 = mod
        h = _ffn_call(x, sh1, sc1, g1, norm_ffn1[0], wts["w1g"], wts["w1u"], wts["w1d"],
                      tm=tm_f, tiles_per_group=tiles_f)
        proj = _inproj_call(h, sh2, sc2, norm_mix[0], wts["wn"], wts["wt"], q_norm[0], k_norm[0],
                            idx_k_norm[0], v_norm[0], tm=tm_p, tiles_per_group=tiles_p)
        return h, proj, (g2, sh3, sc3, g3)

    def finish(h, out_a, u, vb, rest, ws, bs_t, tm_f, tm_p, tiles_f, tiles_p, unit):
        g2, sh3, sc3, g3 = rest
        h2 = _outproj_call(out_a, u, vb, h, g2, ws, bs_t, out_norm_a[0], out_norm_b[0], wts["wo"],
                           tm=tm_p, tiles_per_group=tiles_p, unit=unit)
        return _ffn_call(h2, sh3, sc3, g3, norm_ffn2[0], wts["w2g"], wts["w2u"], wts["w2d"],
                         tm=tm_f, tiles_per_group=tiles_f)

    xp = x_prompt.reshape(batch * seq, d)
    tf_p, tp_p = seq // tm_ffn, seq // tq
    h_p, proj_p, rest_p = stream(xp, mods_p, tm_ffn, tq, tf_p, tp_p)
    qT, qiT, wT, vT, k_p, kb_p, v_p, ki_p, kib_p, u_p, vb_p = proj_p
    out_a_p = _attn_prompt_call(qT, qiT, wT, kb_p, vT, kib_p, batch=batch, seq=seq, tq=tq)
    y_p = finish(h_p, out_a_p, u_p, vb_p, rest_p, w_spatial[0], b_spatial[0].T, tm_ffn, tq, tf_p, tp_p, CHUNK)

    xs = x_sample.reshape(m_s, d)
    h_s, proj_s, rest_s = stream(xs, mods_s, m_s, m_s, 1, 1)
    qT_s, qiT_s, wT_s, _, k_s, kb_s, v_s, ki_s, kib_s, u_s, vb_s = proj_s
    q_rows = qT_s[0].reshape(N_HEADS, HEAD_DIM, nseq, ntok).transpose(2, 0, 3, 1).reshape(
        nseq, N_HEADS * ntok, HEAD_DIM)
    qi2d = qiT_s[0].reshape(IDX_HEADS, IDX_DIM, nseq, ntok).transpose(2, 0, 3, 1).reshape(
        nseq, IDX_HEADS * ntok, IDX_DIM)
    wcol = wT_s[0].reshape(IDX_HEADS, nseq, ntok).transpose(1, 0, 2).reshape(nseq, IDX_HEADS * ntok, 1)

    def pad_rows(a, rows):
        a = a.reshape(nseq, -1, a.shape[-1])
        return jnp.pad(a, ((0, 0), (0, rows - a.shape[1]), (0, 0)))

    bias = _select_decode_call(page_table, qi2d, wcol, pad_rows(kib_s, PAGE_SIZE),
                               cache_kidx[0], pages_per_step=math.gcd(n_pages, 16))
    slope_col = jnp.repeat(_alibi_slopes(), ntok).reshape(N_HEADS * ntok, 1)
    out_a_s = _attn_decode_call(page_table, q_rows, slope_col, bias,
                                pad_rows(kb_s.reshape(m_s * N_HEADS, HEAD_DIM), PAGE_ROWS),
                                pad_rows(v_s.astype(BF16).reshape(m_s * N_HEADS, HEAD_DIM), PAGE_ROWS),
                                cache_k.reshape(n_pool, PAGE_ROWS, HEAD_DIM),
                                cache_v.reshape(n_pool, PAGE_ROWS, HEAD_DIM),
                                pages_per_step=math.gcd(n_pages, 8))
    out_a_s = out_a_s.reshape(nseq, N_HEADS, ntok, HEAD_DIM).transpose(0, 2, 1, 3).reshape(m_s, w_attn)
    ws_s = jnp.einsum("ab,gts->gatbs", jnp.eye(nseq, dtype=F32), w_spatial[0][:, :ntok, :ntok]).reshape(
        N_GROUPS, m_s, m_s)
    bs_s = jnp.tile(b_spatial[0][:, :ntok].T, (nseq, 1))
    y_s = finish(h_s, out_a_s, u_s, vb_s, rest_s, ws_s, bs_s, m_s, m_s, 1, 1, m_s)

    return (
        y_p.reshape(batch, seq, d),
        y_s.reshape(nseq, ntok, d),
        k_p.reshape(1, batch, seq, N_HEADS, HEAD_DIM),
        v_p.reshape(1, batch, seq, N_HEADS, HEAD_DIM),
        ki_p.reshape(1, batch, seq, IDX_DIM),
        k_s.reshape(1, nseq, ntok, N_HEADS, HEAD_DIM),
        v_s.reshape(1, nseq, ntok, N_HEADS, HEAD_DIM),
        ki_s.reshape(1, nseq, ntok, IDX_DIM),
        vb_s.reshape(1, nseq, ntok, w_g),
    )


def kernel(x_prompt, x_sample, cache_k, cache_v, cache_kidx, page_table, c_prompt, c_sample, w_ada, b_ada, norm_ffn1, w1_gate, w1_up, w1_down, norm_mix, w_in, q_norm, k_norm, idx_k_norm, w_spatial, b_spatial, v_norm, out_norm_a, out_norm_b, w_out, norm_ffn2, w2_gate, w2_up, w2_down):
    return _forward(x_prompt, x_sample, cache_k, cache_v, cache_kidx, page_table, c_prompt, c_sample,
                    w_ada, b_ada, norm_ffn1, w1_gate, w1_up, w1_down, norm_mix, w_in, q_norm, k_norm,
                    idx_k_norm, w_spatial, b_spatial, v_norm, out_norm_a, out_norm_b, w_out, norm_ffn2,
                    w2_gate, w2_up, w2_down)
```

```python
import functools
import math

import jax
import jax.numpy as jnp
from jax import lax
from jax.experimental import pallas as pl
from jax.experimental.pallas import tpu as pltpu

F32 = jnp.float32
BF16 = jnp.bfloat16
I32 = jnp.int32

EPS = 1e-6
HEAD_DIM = 128
N_HEADS = 8
N_GROUPS = 8
GROUP_DIM = 128
CHUNK = 128
IDX_HEADS = 16
IDX_DIM = 64
TOPK_MAX = 256
PAGE_SIZE = 128
N_MOD = 9
SUBLANES = 8
PAGE_ROWS = PAGE_SIZE * N_HEADS

INT_MIN = -(2 ** 31)
INT_MAX = 2 ** 31 - 1
F32_MIN = float(jnp.finfo(jnp.float32).min)
LOG2E = math.log2(math.e)

VMEM_LIMIT_BYTES = 60000 * 1024

NT_DIMS = (((1,), (1,)), ((), ()))


def _compiler_params(semantics):
    return pltpu.CompilerParams(dimension_semantics=semantics, vmem_limit_bytes=VMEM_LIMIT_BYTES)


def _modulate(x, gain, shift, scale):
    ms = jnp.mean(x * x, axis=-1, keepdims=True)
    y = x * lax.rsqrt(ms + EPS)
    return (y * gain) * (1.0 + scale) + shift


def _rms_rows(x, gain):
    ms = jnp.mean(x * x, axis=-1, keepdims=True)
    return (x * lax.rsqrt(ms + EPS)) * gain


def _div_pow2(x, n):
    assert n & (n - 1) == 0
    return lax.shift_right_logical(x, jnp.int32(n.bit_length() - 1))


def _mod_pow2(x, n):
    assert n & (n - 1) == 0
    return x & jnp.int32(n - 1)


LANES = 128


def _fold_lane_tiles(x, op):
    return functools.reduce(op, [x[:, i * LANES:(i + 1) * LANES] for i in range(x.shape[1] // LANES)])


def _rms_cols(x, gain_col):
    ms = jnp.mean(x * x, axis=0, keepdims=True)
    return (x * lax.rsqrt(ms + EPS)) * gain_col


def _ada_kernel(c_ref, w_ref, b_ref, o_ref):
    a = jax.nn.silu(c_ref[...]).astype(BF16)
    o_ref[...] = jnp.dot(a, w_ref[...].astype(BF16), preferred_element_type=F32) + b_ref[...]


def _ada_call(c_all, w_ada, b_ada):
    rows, d = c_all.shape
    n = w_ada.shape[1]
    tn = 1024
    return pl.pallas_call(
        _ada_kernel,
        grid=(n // tn,),
        in_specs=[
            pl.BlockSpec((rows, d), lambda j: (0, 0)),
            pl.BlockSpec((d, tn), lambda j: (0, j)),
            pl.BlockSpec((1, tn), lambda j: (0, j)),
        ],
        out_specs=pl.BlockSpec((rows, tn), lambda j: (0, j)),
        out_shape=jax.ShapeDtypeStruct((rows, n), F32),
        compiler_params=_compiler_params(("parallel",)),
        name="ada_proj",
    )(c_all, w_ada, b_ada.reshape(1, n))


def _ffn_kernel(x_ref, sh_ref, sc_ref, g_ref, nw_ref, wg_ref, wu_ref, wd_ref, o_ref, xn_ref):
    n = pl.program_id(1)

    @pl.when(n == 0)
    def _():
        xn_ref[...] = _modulate(x_ref[...], nw_ref[...], sh_ref[...], sc_ref[...]).astype(BF16)
        o_ref[...] = jnp.zeros_like(o_ref)

    xn = xn_ref[...]
    gate = jnp.dot(xn, wg_ref[...].astype(BF16), preferred_element_type=F32)
    up = jnp.dot(xn, wu_ref[...].astype(BF16), preferred_element_type=F32)
    hmid = (jax.nn.silu(gate) * up).astype(BF16)
    o_ref[...] += jnp.dot(hmid, wd_ref[...].astype(BF16), preferred_element_type=F32)

    @pl.when(n == pl.num_programs(1) - 1)
    def _():
        o_ref[...] = x_ref[...] + (0.5 * g_ref[...]) * o_ref[...]


def _mod_spec(mod, tiles_per_group):
    _, r, d = mod.shape
    return pl.BlockSpec((None, r, d), lambda i, *_: (i // tiles_per_group, 0, 0))


def _ffn_call(x, shift, scale, gate, norm_w, wg, wu, wd, *, tm, tiles_per_group):
    m, d = x.shape
    dff = wg.shape[1]
    tn = 256
    return pl.pallas_call(
        _ffn_kernel,
        grid=(m // tm, dff // tn),
        in_specs=[
            pl.BlockSpec((tm, d), lambda i, n: (i, 0), pipeline_mode=pl.Buffered(1)),
            _mod_spec(shift, tiles_per_group),
            _mod_spec(scale, tiles_per_group),
            _mod_spec(gate, tiles_per_group),
            pl.BlockSpec((1, d), lambda i, n: (0, 0)),
            pl.BlockSpec((d, tn), lambda i, n: (0, n)),
            pl.BlockSpec((d, tn), lambda i, n: (0, n)),
            pl.BlockSpec((tn, d), lambda i, n: (n, 0)),
        ],
        out_specs=pl.BlockSpec((tm, d), lambda i, n: (i, 0)),
        out_shape=jax.ShapeDtypeStruct((m, d), F32),
        scratch_shapes=[pltpu.VMEM((tm, d), BF16)],
        compiler_params=_compiler_params(("parallel", "arbitrary")),
        name="ffn",
    )(x, shift, scale, gate, norm_w.reshape(1, d), wg, wu, wd)


W_ATTN = N_HEADS * HEAD_DIM
W_IDX = IDX_HEADS * IDX_DIM
W_GATE = N_GROUPS * GROUP_DIM
OFF_Q, OFF_K, OFF_V = 0, W_ATTN, 2 * W_ATTN
OFF_QI = 3 * W_ATTN
OFF_KIW = OFF_QI + W_IDX
OFF_U = OFF_KIW + LANES
OFF_VB = OFF_U + W_GATE
W_IN_PACKED = OFF_VB + W_GATE


def _inproj_kernel(h_ref, sh_ref, sc_ref, nw_ref, w_ref, qn_ref, kn_ref, ikn_ref, vn_ref, *out_refs, decode):
    if decode:
        q_ref, k_ref, kb_ref, v_ref, u_ref, vb_ref = out_refs
    else:
        qT_ref, qiT_ref, wT_ref, vT_ref, k_ref, kb_ref, v_ref, ki_ref, kib_ref, u_ref, vb_ref = out_refs
    a = _modulate(h_ref[...], nw_ref[...], sh_ref[...], sc_ref[...]).astype(BF16)

    def proj(off, width):
        return jnp.dot(a, w_ref[:, off:off + width], preferred_element_type=F32)

    q = proj(OFF_Q, W_ATTN)
    k = proj(OFF_K, W_ATTN)
    for h in range(N_HEADS):
        cols = slice(h * HEAD_DIM, (h + 1) * HEAD_DIM)
        qh = _rms_rows(q[:, cols], qn_ref[...])
        if decode:
            q_ref[:, cols] = qh.astype(BF16)
        else:
            qT_ref[cols, :] = qh.T.astype(BF16)
        kh = _rms_rows(k[:, cols], kn_ref[...])
        k_ref[:, cols] = kh
        kb_ref[:, cols] = kh.astype(BF16)
    v = proj(OFF_V, W_ATTN)
    v_ref[...] = v
    u_ref[...] = jax.nn.gelu(proj(OFF_U, W_GATE))
    vb = jax.nn.gelu(proj(OFF_VB, W_GATE))
    for g in range(N_GROUPS):
        cols = slice(g * GROUP_DIM, (g + 1) * GROUP_DIM)
        vb_ref[:, cols] = _rms_rows(vb[:, cols], vn_ref[g:g + 1, :])
    if not decode:
        vT_ref[...] = v.T.astype(BF16)
        qiT_ref[...] = (proj(OFF_QI, W_IDX) * (IDX_DIM ** -0.5)).T.astype(BF16)
        kiw = proj(OFF_KIW, LANES)
        ki = _rms_rows(kiw[:, 0:IDX_DIM], ikn_ref[...])
        ki_ref[...] = ki
        kib_ref[...] = ki.astype(BF16)
        wT_ref[...] = kiw.T[IDX_DIM:IDX_DIM + IDX_HEADS, :] * (IDX_HEADS ** -0.5)


def _inproj_call(h, shift, scale, norm_w, w_packed, q_norm, k_norm, idx_k_norm, v_norm, *, tm, tiles_per_group,
                 decode):
    m, d = h.shape
    nt = m // tm
    row = lambda i: (i, 0)
    fm = lambda i: (i, 0, 0)
    const = lambda i: (0, 0)
    token_major = lambda width, dtype: (jax.ShapeDtypeStruct((m, width), dtype), pl.BlockSpec((tm, width), row))
    feature_major = lambda width, dtype: (jax.ShapeDtypeStruct((nt, width, tm), dtype),
                                          pl.BlockSpec((None, width, tm), fm))
    if decode:
        outs = [token_major(W_ATTN, BF16)]
    else:
        outs = [feature_major(W_ATTN, BF16), feature_major(W_IDX, BF16),
                feature_major(IDX_HEADS, F32), feature_major(W_ATTN, BF16)]
    outs += [token_major(W_ATTN, F32), token_major(W_ATTN, BF16), token_major(W_ATTN, F32)]
    if not decode:
        outs += [token_major(IDX_DIM, F32), token_major(IDX_DIM, BF16)]
    outs += [token_major(W_GATE, F32), token_major(W_GATE, F32)]
    return pl.pallas_call(
        functools.partial(_inproj_kernel, decode=decode),
        grid=(nt,),
        in_specs=[
            pl.BlockSpec((tm, d), row),
            _mod_spec(shift, tiles_per_group),
            _mod_spec(scale, tiles_per_group),
            pl.BlockSpec((1, d), const),
            pl.BlockSpec(w_packed.shape, const),
            pl.BlockSpec((1, HEAD_DIM), const),
            pl.BlockSpec((1, HEAD_DIM), const),
            pl.BlockSpec((1, IDX_DIM), const),
            pl.BlockSpec((N_GROUPS, GROUP_DIM), const),
        ],
        out_specs=[spec for _, spec in outs],
        out_shape=[shape for shape, _ in outs],
        compiler_params=_compiler_params(("parallel",)),
        name="in_proj",
    )(h, shift, scale, norm_w.reshape(1, d), w_packed, q_norm.reshape(1, HEAD_DIM), k_norm.reshape(1, HEAD_DIM),
      idx_k_norm.reshape(1, IDX_DIM), v_norm)


def _dot3(a, b, dims=(((1,), (0,)), ((), ()))):
    a_hi = a.astype(BF16)
    b_hi = b.astype(BF16)
    a_lo = (a - a_hi.astype(F32)).astype(BF16)
    b_lo = (b - b_hi.astype(F32)).astype(BF16)
    dot = functools.partial(lax.dot_general, dimension_numbers=dims, preferred_element_type=F32)
    return dot(a_hi, b_hi) + (dot(a_lo, b_hi) + dot(a_hi, b_lo))


def _idxproj_kernel(h_ref, sh_ref, sc_ref, nw_ref, w_ref, ikn_ref, qi_ref, ki_ref, wi_ref):
    a = _modulate(h_ref[...], nw_ref[...], sh_ref[...], sc_ref[...])
    r = _dot3(a, w_ref[...])
    qi_ref[...] = r[:, 0:W_IDX] * (IDX_DIM ** -0.5)
    kiw = r[:, W_IDX:W_IDX + LANES]
    ki_ref[...] = _rms_rows(kiw[:, 0:IDX_DIM], ikn_ref[...])
    wi_ref[...] = kiw[:, IDX_DIM:IDX_DIM + IDX_HEADS] * (IDX_HEADS ** -0.5)


def _idxproj_call(h, shift, scale, norm_w, w_idx, idx_k_norm):
    m, d = h.shape
    full = lambda a: pl.BlockSpec(a.shape, lambda i: (0,) * a.ndim)
    args = (h, shift[0], scale[0], norm_w.reshape(1, d), w_idx, idx_k_norm.reshape(1, IDX_DIM))
    return pl.pallas_call(
        _idxproj_kernel,
        grid=(1,),
        in_specs=[full(a) for a in args],
        out_specs=[pl.BlockSpec((m, W_IDX), lambda i: (0, 0)), pl.BlockSpec((m, IDX_DIM), lambda i: (0, 0)),
                   pl.BlockSpec((m, IDX_HEADS), lambda i: (0, 0))],
        out_shape=[jax.ShapeDtypeStruct((m, W_IDX), F32), jax.ShapeDtypeStruct((m, IDX_DIM), F32),
                   jax.ShapeDtypeStruct((m, IDX_HEADS), F32)],
        compiler_params=_compiler_params(("arbitrary",)),
        name="idx_proj_decode",
    )(*args)


def _key_to_float(key):
    return lax.bitcast_convert_type(jnp.where(key < 0, key ^ jnp.int32(INT_MAX), key), F32)


def _topk_threshold(count_ge, topk, like):
    zero = jnp.zeros_like(like)
    cnt0 = count_ge(_key_to_float(zero))
    nonneg = cnt0 >= topk
    t0 = jnp.where(nonneg, zero, jnp.int32(INT_MIN))
    c0 = jnp.where(nonneg, cnt0, jnp.int32(topk))

    def body(p, carry):
        t, ct = carry
        cand = t | jnp.left_shift(jnp.int32(1), 31 - p)
        cnt = count_ge(_key_to_float(cand))
        take = cnt >= topk
        return jnp.where(take, cand, t), jnp.where(take, cnt, ct)

    t, ct = lax.fori_loop(1, 32, body, (t0, c0))
    return t != jnp.int32(INT_MIN), _key_to_float(t), ct


def _tie_cutoff(count_where, thr, topk, nbits, excess):
    need = topk - count_where(lambda sc, pos: sc > thr)
    j = jnp.zeros_like(need)
    for bit in reversed(range(nbits)):
        cand = j | jnp.int32(1 << bit)
        cnt = count_where(lambda sc, pos, cand=cand: (sc == thr) & (pos < cand))
        j = jnp.where(cnt < need, cand, j)
    return jnp.where(excess, j, jnp.int32(INT_MAX))


def _select(count_where, topk, pos_bits, like):
    has_thr, thr, cnt_thr = _topk_threshold(lambda t: count_where(lambda sc, pos: sc >= t), topk, like)
    excess = (cnt_thr > topk) & has_thr
    cutoff = lax.cond(
        jnp.max(excess.astype(I32)) > 0,
        lambda: _tie_cutoff(count_where, thr, topk, pos_bits, excess),
        lambda: jnp.full(like.shape, INT_MAX, I32),
    )
    return lambda sc, pos: jnp.logical_not(has_thr) | (sc > thr) | ((sc == thr) & (pos <= cutoff))


def _attn_prompt_kernel(qT_ref, qiT_ref, wT_ref, kb_ref, vT_ref, kib_ref, o_ref,
                        sc_ref, dm_ref, acc_ref, m_ref, l_ref, a_ref, s_ref, p_ref, *, tq, topk, pos_bits):
    j = pl.program_id(1)
    nchunks = j + 1
    groups = tq // SUBLANES
    t_pos = j * tq + lax.broadcasted_iota(I32, (1, tq), 1)
    s_off = lax.broadcasted_iota(I32, (tq, 1), 0)

    def chunk_start(c):
        return pl.multiple_of(c * tq, tq)

    def score_body(c, carry):
        off = chunk_start(c)
        kic = kib_ref[pl.ds(off, tq), :]
        sc = jnp.zeros((tq, tq), F32)
        for h in range(IDX_HEADS):
            d = jnp.dot(kic, qiT_ref[h * IDX_DIM:(h + 1) * IDX_DIM, :], preferred_element_type=F32)
            sc = sc + jnp.maximum(d, 0.0) * wT_ref[h:h + 1, :]
        sc_ref[pl.ds(off, tq), :] = jnp.where((off + s_off) <= t_pos, sc, -jnp.inf)
        return carry

    lax.fori_loop(0, nchunks, score_body, 0)

    def count_where(pred):
        def body(c, cnt):
            off = chunk_start(c)
            hit = pred(sc_ref[pl.ds(off, tq), :], off + s_off).reshape(groups, SUBLANES, tq)
            accs = [cnt] + [jnp.zeros_like(cnt)] * 3
            for g in range(groups):
                accs[g % 4] = jnp.where(hit[g], accs[g % 4] + 1, accs[g % 4])
            return (accs[0] + accs[1]) + (accs[2] + accs[3])

        cnt8 = lax.fori_loop(0, nchunks, body, jnp.zeros((SUBLANES, tq), I32))
        return jnp.sum(cnt8, axis=0, keepdims=True)

    selected = _select(count_where, topk, pos_bits, t_pos)

    def dm_body(c, carry):
        off = chunk_start(c)
        pos = off + s_off
        sel = selected(sc_ref[pl.ds(off, tq), :], pos) & (pos <= t_pos)
        dm_ref[pl.ds(off, tq), :] = jnp.where(sel, (t_pos - pos).astype(F32) * LOG2E, jnp.inf)
        return carry

    lax.fori_loop(0, nchunks, dm_body, 0)

    m_ref[...] = jnp.full(m_ref.shape, F32_MIN, F32)
    l_ref[...] = jnp.zeros_like(l_ref)
    acc_ref[...] = jnp.zeros_like(acc_ref)
    qk_scale = (HEAD_DIM ** -0.5) * LOG2E

    def att_body(c, carry):
        off = chunk_start(c)
        dmc = dm_ref[pl.ds(off, tq), :]
        for h in range(N_HEADS):
            rows = slice(h * HEAD_DIM, (h + 1) * HEAD_DIM)
            qk = jnp.dot(kb_ref[pl.ds(off, tq), rows], qT_ref[rows, :], preferred_element_type=F32)
            s = qk * qk_scale - (2.0 ** -(h + 1)) * dmc
            s_ref[h] = s
            cm = jnp.max(s.reshape(groups, SUBLANES, tq), axis=0)
            m_old = m_ref[h]
            m_new = jnp.maximum(m_old, jnp.broadcast_to(jnp.max(cm, axis=0, keepdims=True), cm.shape))
            a_ref[h] = jnp.exp2(m_old - m_new)
            m_ref[h] = m_new
        for h in range(N_HEADS):
            p = jnp.exp2(s_ref[h].reshape(groups, SUBLANES, tq) - m_ref[h][None])
            l_ref[h] = a_ref[h] * l_ref[h] + jnp.sum(p, axis=0)
            p_ref[h] = p.reshape(tq, tq).astype(BF16)
        for h in range(N_HEADS):
            rows = slice(h * HEAD_DIM, (h + 1) * HEAD_DIM)
            pv = jnp.dot(vT_ref[c, rows, :], p_ref[h], preferred_element_type=F32)
            acc = acc_ref[rows, :].reshape(HEAD_DIM // SUBLANES, SUBLANES, tq) * a_ref[h][None]
            acc_ref[rows, :] = acc.reshape(HEAD_DIM, tq) + pv
        return carry

    lax.fori_loop(0, nchunks, att_body, 0)

    for h in range(N_HEADS):
        rows = slice(h * HEAD_DIM, (h + 1) * HEAD_DIM)
        acc_ref[rows, :] = acc_ref[rows, :] / jnp.sum(l_ref[h], axis=0, keepdims=True)
    o_ref[...] = acc_ref[...].T


def _attn_prompt_call(qT, qiT, wT, kb, vT, kib, *, batch, seq, tq):
    w_attn = N_HEADS * HEAD_DIM
    w_idx = IDX_HEADS * IDX_DIM
    nq = seq // tq
    topk = min(TOPK_MAX, seq // 4)
    qblk = lambda b, j: (b * nq + j, 0, 0)
    kernel = functools.partial(_attn_prompt_kernel, tq=tq, topk=topk, pos_bits=max(1, (seq - 1).bit_length()))
    return pl.pallas_call(
        kernel,
        grid=(batch, nq),
        in_specs=[
            pl.BlockSpec((None, w_attn, tq), qblk),
            pl.BlockSpec((None, w_idx, tq), qblk),
            pl.BlockSpec((None, IDX_HEADS, tq), qblk),
            pl.BlockSpec((seq, w_attn), lambda b, j: (b, 0), pipeline_mode=pl.Buffered(1)),
            pl.BlockSpec((nq, w_attn, tq), lambda b, j: (b, 0, 0), pipeline_mode=pl.Buffered(1)),
            pl.BlockSpec((seq, IDX_DIM), lambda b, j: (b, 0), pipeline_mode=pl.Buffered(1)),
        ],
        out_specs=pl.BlockSpec((tq, w_attn), lambda b, j: (b * nq + j, 0)),
        out_shape=jax.ShapeDtypeStruct((batch * seq, w_attn), F32),
        scratch_shapes=[
            pltpu.VMEM((seq, tq), F32),
            pltpu.VMEM((seq, tq), F32),
            pltpu.VMEM((w_attn, tq), F32),
            pltpu.VMEM((N_HEADS, SUBLANES, tq), F32),
            pltpu.VMEM((N_HEADS, SUBLANES, tq), F32),
            pltpu.VMEM((N_HEADS, SUBLANES, tq), F32),
            pltpu.VMEM((N_HEADS, tq, tq), F32),
            pltpu.VMEM((N_HEADS, tq, tq), BF16),
        ],
        compiler_params=_compiler_params(("parallel", "arbitrary")),
        name="attn_prompt",
    )(qT, qiT, wT, kb, vT, kib)


def _outproj_kernel(oa_ref, u_ref, vb_ref, h_ref, g_ref, ws_ref, bs_ref, na_ref, nb_ref, wo_ref, o_ref, *, unit):
    tm = oa_ref.shape[0]
    w_attn = N_HEADS * HEAD_DIM
    rows_i = lax.broadcasted_iota(I32, (unit, unit), 0)
    cols_i = lax.broadcasted_iota(I32, (unit, unit), 1)
    causal = rows_i >= cols_i
    ob_parts = []
    for g in range(N_GROUPS):
        cols = slice(g * GROUP_DIM, (g + 1) * GROUP_DIM)
        wm = jnp.where(causal, ws_ref[g], 0.0).astype(BF16)
        bias = bs_ref[:, g:g + 1]
        mixed = []
        for r in range(tm // unit):
            rws = slice(r * unit, (r + 1) * unit)
            mixed.append(jnp.dot(wm, vb_ref[rws, cols].astype(BF16), preferred_element_type=F32) + bias)
        mixed = mixed[0] if len(mixed) == 1 else jnp.concatenate(mixed, axis=0)
        ob_parts.append(u_ref[:, cols] * mixed)
    ob = jnp.concatenate(ob_parts, axis=1)
    za = _rms_rows(oa_ref[...], na_ref[...]).astype(BF16)
    zb = _rms_rows(ob, nb_ref[...]).astype(BF16)
    mix = (jnp.dot(za, wo_ref[0:w_attn, :], preferred_element_type=F32)
           + jnp.dot(zb, wo_ref[w_attn:, :], preferred_element_type=F32))
    o_ref[...] = h_ref[...] + g_ref[...] * mix


def _outproj_call(out_a, u, vb, h, gate, ws, bs_t, norm_a, norm_b, w_out, *, tm, tiles_per_group, unit):
    m, d = h.shape
    w_attn = out_a.shape[1]
    w_g = u.shape[1]
    row = lambda i: (i, 0)
    const = lambda i: (0, 0)
    return pl.pallas_call(
        functools.partial(_outproj_kernel, unit=unit),
        grid=(m // tm,),
        in_specs=[
            pl.BlockSpec((tm, w_attn), row),
            pl.BlockSpec((tm, w_g), row),
            pl.BlockSpec((tm, w_g), row),
            pl.BlockSpec((tm, d), row),
            _mod_spec(gate, tiles_per_group),
            pl.BlockSpec(ws.shape, lambda i: (0, 0, 0)),
            pl.BlockSpec(bs_t.shape, const),
            pl.BlockSpec((1, w_attn), const),
            pl.BlockSpec((1, w_g), const),
            pl.BlockSpec(w_out.shape, const),
        ],
        out_specs=pl.BlockSpec((tm, d), row),
        out_shape=jax.ShapeDtypeStruct((m, d), F32),
        compiler_params=_compiler_params(("parallel",)),
        name="out_proj",
    )(out_a, u, vb, h, gate, ws, bs_t, norm_a.reshape(1, w_attn), norm_b.reshape(1, w_g), w_out)


def _select_decode_kernel(pt_ref, qi_ref, w_ref, kin_ref, *rest, pages_per_step, n_pages, topk, pos_bits):
    page_refs = rest[:pages_per_step]
    bias_ref = rest[pages_per_step]
    sc_ref = rest[pages_per_step + 1]
    p = pl.program_id(1)
    qi = qi_ref[...]
    w = w_ref[...]

    def page_scores(keys):
        d = _dot3(qi, keys, NT_DIMS)
        r = jnp.maximum(d, 0.0) * w
        return jnp.sum(r.reshape(IDX_HEADS, SUBLANES, PAGE_SIZE), axis=0)

    for i in range(pages_per_step):
        sc_ref[p * pages_per_step + i] = page_scores(page_refs[i][...])

    @pl.when(p == pl.num_programs(1) - 1)
    def _():
        shape = (n_pages + 1, SUBLANES, PAGE_SIZE)
        tok = lax.broadcasted_iota(I32, (SUBLANES, PAGE_SIZE), 0)
        lane = lax.broadcasted_iota(I32, (SUBLANES, PAGE_SIZE), 1)
        sc_ref[n_pages] = jnp.where(lane <= tok, page_scores(kin_ref[...]), -jnp.inf)
        pos = lax.broadcasted_iota(I32, shape, 0) * PAGE_SIZE + lax.broadcasted_iota(I32, shape, 2)
        q_pos = n_pages * PAGE_SIZE + lax.broadcasted_iota(I32, shape, 1)

        def count_where(pred):
            hit = pred(sc_ref[...], pos).astype(I32)
            return jnp.sum(jnp.sum(hit, axis=0), axis=-1, keepdims=True)

        selected = _select(count_where, topk, pos_bits, jnp.zeros((SUBLANES, 1), I32))
        sel = selected(sc_ref[...], pos) & (pos <= q_pos)
        sel2d = jnp.where(sel, 1.0, 0.0).astype(BF16).reshape((n_pages + 1) * SUBLANES, PAGE_SIZE)
        expand = (_div_pow2(lax.broadcasted_iota(I32, (PAGE_SIZE, PAGE_ROWS), 1), N_HEADS)
                  == lax.broadcasted_iota(I32, (PAGE_SIZE, PAGE_ROWS), 0))
        rep = jnp.dot(sel2d, jnp.where(expand, 1.0, 0.0).astype(BF16), preferred_element_type=F32)
        bias_ref[...] = jnp.where(rep > 0.5, 0.0, -jnp.inf).reshape(n_pages + 1, SUBLANES, PAGE_ROWS)


def _select_decode_call(page_table, qi2d, wcol, ki_new, cache_kidx, *, pages_per_step):
    nseq, n_pages = page_table.shape
    topk = min(TOPK_MAX, (n_pages * PAGE_SIZE + SUBLANES) // 4)
    pos_bits = max(1, ((n_pages + 1) * PAGE_SIZE - 1).bit_length())
    per_seq = lambda b, p, pt: (b, 0, 0)

    def page_spec(i):
        return pl.BlockSpec((None, PAGE_SIZE, IDX_DIM),
                            lambda b, p, pt, i=i: (pt[b, p * pages_per_step + i], 0, 0))

    kernel = functools.partial(_select_decode_kernel, pages_per_step=pages_per_step, n_pages=n_pages,
                               topk=topk, pos_bits=pos_bits)
    grid_spec = pltpu.PrefetchScalarGridSpec(
        num_scalar_prefetch=1,
        grid=(nseq, n_pages // pages_per_step),
        in_specs=[
            pl.BlockSpec((None, IDX_HEADS * SUBLANES, IDX_DIM), per_seq),
            pl.BlockSpec((None, IDX_HEADS * SUBLANES, 1), per_seq),
            pl.BlockSpec((None, PAGE_SIZE, IDX_DIM), per_seq),
        ] + [page_spec(i) for i in range(pages_per_step)],
        out_specs=pl.BlockSpec((None, n_pages + 1, SUBLANES, PAGE_ROWS), lambda b, p, pt: (b, 0, 0, 0)),
        scratch_shapes=[pltpu.VMEM((n_pages + 1, SUBLANES, PAGE_SIZE), F32)],
    )
    return pl.pallas_call(
        kernel,
        grid_spec=grid_spec,
        out_shape=jax.ShapeDtypeStruct((nseq, n_pages + 1, SUBLANES, PAGE_ROWS), F32),
        compiler_params=_compiler_params(("parallel", "arbitrary")),
        name="select_decode",
    )(page_table, qi2d, wcol, ki_new, *([cache_kidx] * pages_per_step))


def _attn_decode_kernel(pt_ref, q_ref, slope_ref, bias_ref, kn_ref, vn_ref, *rest, pages_per_step, n_pages):
    k_refs = rest[:pages_per_step]
    v_refs = rest[pages_per_step:2 * pages_per_step]
    o_ref, a_ref, acc_ref, m_ref, l_ref = rest[2 * pages_per_step:]
    p = pl.program_id(1)
    nrow = N_HEADS * SUBLANES
    qk_scale = (HEAD_DIM ** -0.5) * LOG2E
    slope = slope_ref[...] * LOG2E

    @pl.when(p == 0)
    def _():
        m_ref[...] = jnp.full(m_ref.shape, F32_MIN, F32)
        l_ref[...] = jnp.zeros_like(l_ref)
        acc_ref[...] = jnp.zeros_like(acc_ref)
        row = lax.broadcasted_iota(I32, (nrow, PAGE_ROWS), 0)
        col = lax.broadcasted_iota(I32, (nrow, PAGE_ROWS), 1)
        same_head = _mod_pow2(col, N_HEADS) == _div_pow2(row, SUBLANES)
        tok_minus_key = (_mod_pow2(row, SUBLANES) - _div_pow2(col, N_HEADS)).astype(F32)
        a_ref[...] = jnp.where(same_head, -slope * tok_minus_key, -jnp.inf)

    def logits(keys_bf16, page):
        page_dist = jnp.full((nrow, 1), (n_pages - page) * PAGE_SIZE, I32).astype(F32)
        s = lax.dot_general(q_ref[...], keys_bf16, NT_DIMS, preferred_element_type=F32)
        return s * qk_scale + a_ref[...] - slope * page_dist + jnp.concatenate([bias_ref[page]] * N_HEADS, axis=0)

    def update(s_pages, v_pages):
        lanes_max = functools.reduce(jnp.maximum, [_fold_lane_tiles(s, jnp.maximum) for s in s_pages])
        m_old = m_ref[...]
        m_new = jnp.maximum(m_old, jnp.max(lanes_max, axis=-1, keepdims=True))
        alpha = jnp.exp2(m_old - m_new)
        lanes_sum = jnp.zeros((nrow, PAGE_SIZE), F32)
        acc = alpha * acc_ref[...]
        for s, vals in zip(s_pages, v_pages):
            pr = jnp.exp2(s - m_new)
            lanes_sum = lanes_sum + _fold_lane_tiles(pr, jnp.add)
            acc = acc + jnp.dot(pr.astype(BF16), vals, preferred_element_type=F32)
        l_ref[...] = alpha * l_ref[...] + jnp.sum(lanes_sum, axis=-1, keepdims=True)
        acc_ref[...] = acc
        m_ref[...] = m_new

    update([logits(k_refs[i][...].astype(BF16), p * pages_per_step + i) for i in range(pages_per_step)],
           [v_refs[i][...].astype(BF16) for i in range(pages_per_step)])

    @pl.when(p == pl.num_programs(1) - 1)
    def _():
        update([logits(kn_ref[...], n_pages)], [vn_ref[...]])
        o_ref[...] = acc_ref[...] / l_ref[...]


def _attn_decode_call(page_table, q_rows, slope_col, bias, k_new, v_new, cache_k, cache_v, *, pages_per_step):
    nseq, n_pages = page_table.shape
    nrow = N_HEADS * SUBLANES
    per_seq = lambda b, p, pt: (b, 0, 0)

    def page_spec(i):
        return pl.BlockSpec((None, PAGE_ROWS, HEAD_DIM),
                            lambda b, p, pt, i=i: (pt[b, p * pages_per_step + i], 0, 0))

    kernel = functools.partial(_attn_decode_kernel, pages_per_step=pages_per_step, n_pages=n_pages)
    grid_spec = pltpu.PrefetchScalarGridSpec(
        num_scalar_prefetch=1,
        grid=(nseq, n_pages // pages_per_step),
        in_specs=[
            pl.BlockSpec((None, nrow, HEAD_DIM), per_seq),
            pl.BlockSpec((nrow, 1), lambda b, p, pt: (0, 0)),
            pl.BlockSpec((None, n_pages + 1, SUBLANES, PAGE_ROWS), lambda b, p, pt: (b, 0, 0, 0)),
            pl.BlockSpec((None, PAGE_ROWS, HEAD_DIM), per_seq),
            pl.BlockSpec((None, PAGE_ROWS, HEAD_DIM), per_seq),
        ] + [page_spec(i) for i in range(pages_per_step)] + [page_spec(i) for i in range(pages_per_step)],
        out_specs=pl.BlockSpec((None, nrow, HEAD_DIM), per_seq),
        scratch_shapes=[
            pltpu.VMEM((nrow, PAGE_ROWS), F32),
            pltpu.VMEM((nrow, HEAD_DIM), F32),
            pltpu.VMEM((nrow, 1), F32),
            pltpu.VMEM((nrow, 1), F32),
        ],
    )
    return pl.pallas_call(
        kernel,
        grid_spec=grid_spec,
        out_shape=jax.ShapeDtypeStruct((nseq, nrow, HEAD_DIM), F32),
        compiler_params=_compiler_params(("parallel", "arbitrary")),
        name="attn_decode",
    )(page_table, q_rows, slope_col, bias, k_new, v_new,
      *([cache_k] * pages_per_step), *([cache_v] * pages_per_step))


def _pack_in_weights(w_in):
    n_idx_tail = IDX_DIM + IDX_HEADS
    pad = jnp.zeros((w_in.shape[0], LANES - n_idx_tail), w_in.dtype)
    split = OFF_KIW + n_idx_tail
    w_packed = jnp.concatenate([w_in[:, :split], pad, w_in[:, split:]], axis=1).astype(BF16)
    w_idx = jnp.concatenate([w_in[:, OFF_QI:split], pad], axis=1)
    assert w_packed.shape[1] == W_IN_PACKED
    return w_packed, w_idx


def _alibi_slopes():
    return jnp.exp2(-8.0 * jnp.arange(1, N_HEADS + 1, dtype=F32) / N_HEADS)


def _forward(x_prompt, x_sample, cache_k, cache_v, cache_kidx, page_table, c_prompt, c_sample,
             w_ada, b_ada, norm_ffn1, w1_gate, w1_up, w1_down, norm_mix, w_in, q_norm, k_norm,
             idx_k_norm, w_spatial, b_spatial, v_norm, out_norm_a, out_norm_b, w_out, norm_ffn2,
             w2_gate, w2_up, w2_down):
    batch, seq, d = x_prompt.shape
    nseq, ntok, _ = x_sample.shape
    n_pages = page_table.shape[1]
    n_pool = cache_k.shape[1]
    w_attn = N_HEADS * HEAD_DIM
    w_g = N_GROUPS * GROUP_DIM
    assert ntok == SUBLANES and seq % 256 == 0 and cache_k.shape[0] == 1
    tq = 256
    tm_ffn = next(t for t in (1024, 512, 256) if seq % t == 0)
    m_s = nseq * ntok

    w_packed, w_idx = _pack_in_weights(w_in[0])
    wo = w_out[0].astype(BF16)

    n_c = batch + nseq
    c_all = jnp.concatenate([c_prompt, c_sample, jnp.zeros((-n_c % 8, d), F32)], axis=0)
    mods = _ada_call(c_all, w_ada[0], b_ada[0])
    mods_p = [mods[:batch, i * d:(i + 1) * d].reshape(batch, 1, d) for i in range(N_MOD)]
    mods_s = [jnp.repeat(mods[batch:n_c, i * d:(i + 1) * d], ntok, axis=0).reshape(1, m_s, d) for i in range(N_MOD)]

    def stream(x, mod, tm_f, tm_p, tiles_f, tiles_p, decode):
        sh1, sc1, g1, sh2, sc2, g2, sh3, sc3, g3 = mod
        h = _ffn_call(x, sh1, sc1, g1, norm_ffn1[0], w1_gate[0], w1_up[0], w1_down[0],
                      tm=tm_f, tiles_per_group=tiles_f)
        proj = _inproj_call(h, sh2, sc2, norm_mix[0], w_packed, q_norm[0], k_norm[0],
                            idx_k_norm[0], v_norm[0], tm=tm_p, tiles_per_group=tiles_p, decode=decode)
        return h, proj, (g2, sh3, sc3, g3)

    def finish(h, out_a, u, vb, rest, ws, bs_t, tm_f, tm_p, tiles_f, tiles_p, unit):
        g2, sh3, sc3, g3 = rest
        h2 = _outproj_call(out_a, u, vb, h, g2, ws, bs_t, out_norm_a[0], out_norm_b[0], wo,
                           tm=tm_p, tiles_per_group=tiles_p, unit=unit)
        return _ffn_call(h2, sh3, sc3, g3, norm_ffn2[0], w2_gate[0], w2_up[0], w2_down[0],
                         tm=tm_f, tiles_per_group=tiles_f)

    xp = x_prompt.reshape(batch * seq, d)
    tf_p, tp_p = seq // tm_ffn, seq // tq
    h_p, proj_p, rest_p = stream(xp, mods_p, tm_ffn, tq, tf_p, tp_p, False)
    qT, qiT, wT, vT, k_p, kb_p, v_p, ki_p, kib_p, u_p, vb_p = proj_p
    out_a_p = _attn_prompt_call(qT, qiT, wT, kb_p, vT, kib_p, batch=batch, seq=seq, tq=tq)
    y_p = finish(h_p, out_a_p, u_p, vb_p, rest_p, w_spatial[0], b_spatial[0].T, tm_ffn, tq, tf_p, tp_p, CHUNK)

    xs = x_sample.reshape(m_s, d)
    h_s, proj_s, rest_s = stream(xs, mods_s, m_s, m_s, 1, 1, True)
    q_s, k_s, kb_s, v_s, u_s, vb_s = proj_s
    qi_s, ki_s, wi_s = _idxproj_call(h_s, mods_s[3], mods_s[4], norm_mix[0], w_idx, idx_k_norm[0])
    q_rows = q_s.reshape(nseq, ntok, N_HEADS, HEAD_DIM).transpose(0, 2, 1, 3).reshape(
        nseq, N_HEADS * ntok, HEAD_DIM)
    qi2d = qi_s.reshape(nseq, ntok, IDX_HEADS, IDX_DIM).transpose(0, 2, 1, 3).reshape(
        nseq, IDX_HEADS * ntok, IDX_DIM)
    wcol = wi_s.reshape(nseq, ntok, IDX_HEADS).transpose(0, 2, 1).reshape(nseq, IDX_HEADS * ntok, 1)

    def pad_rows(a, rows):
        a = a.reshape(nseq, -1, a.shape[-1])
        return jnp.pad(a, ((0, 0), (0, rows - a.shape[1]), (0, 0)))

    bias = _select_decode_call(page_table, qi2d, wcol, pad_rows(ki_s, PAGE_SIZE),
                               cache_kidx[0], pages_per_step=math.gcd(n_pages, 16))
    slope_col = jnp.repeat(_alibi_slopes(), ntok).reshape(N_HEADS * ntok, 1)
    out_a_s = _attn_decode_call(page_table, q_rows, slope_col, bias,
                                pad_rows(kb_s.reshape(m_s * N_HEADS, HEAD_DIM), PAGE_ROWS),
                                pad_rows(v_s.astype(BF16).reshape(m_s * N_HEADS, HEAD_DIM), PAGE_ROWS),
                                cache_k.reshape(n_pool, PAGE_ROWS, HEAD_DIM),
                                cache_v.reshape(n_pool, PAGE_ROWS, HEAD_DIM),
                                pages_per_step=math.gcd(n_pages, 8))
    out_a_s = out_a_s.reshape(nseq, N_HEADS, ntok, HEAD_DIM).transpose(0, 2, 1, 3).reshape(m_s, w_attn)
    ws_s = jnp.einsum("ab,gts->gatbs", jnp.eye(nseq, dtype=F32), w_spatial[0][:, :ntok, :ntok]).reshape(
        N_GROUPS, m_s, m_s)
    bs_s = jnp.tile(b_spatial[0][:, :ntok].T, (nseq, 1))
    y_s = finish(h_s, out_a_s, u_s, vb_s, rest_s, ws_s, bs_s, m_s, m_s, 1, 1, m_s)

    return (
        y_p.reshape(batch, seq, d),
        y_s.reshape(nseq, ntok, d),
        k_p.reshape(1, batch, seq, N_HEADS, HEAD_DIM),
        v_p.reshape(1, batch, seq, N_HEADS, HEAD_DIM),
        ki_p.reshape(1, batch, seq, IDX_DIM),
        k_s.reshape(1, nseq, ntok, N_HEADS, HEAD_DIM),
        v_s.reshape(1, nseq, ntok, N_HEADS, HEAD_DIM),
        ki_s.reshape(1, nseq, ntok, IDX_DIM),
        vb_s.reshape(1, nseq, ntok, w_g),
    )


def kernel(x_prompt, x_sample, cache_k, cache_v, cache_kidx, page_table, c_prompt, c_sample, w_ada, b_ada, norm_ffn1, w1_gate, w1_up, w1_down, norm_mix, w_in, q_norm, k_norm, idx_k_norm, w_spatial, b_spatial, v_norm, out_norm_a, out_norm_b, w_out, norm_ffn2, w2_gate, w2_up, w2_down):
    return _forward(x_prompt, x_sample, cache_k, cache_v, cache_kidx, page_table, c_prompt, c_sample,
                    w_ada, b_ada, norm_ffn1, w1_gate, w1_up, w1_down, norm_mix, w_in, q_norm, k_norm,
                    idx_k_norm, w_spatial, b_spatial, v_norm, out_norm_a, out_norm_b, w_out, norm_ffn2,
                    w2_gate, w2_up, w2_down)
```

```python
import functools
import math

import jax
import jax.numpy as jnp
from jax import lax
from jax.experimental import pallas as pl
from jax.experimental.pallas import tpu as pltpu

F32 = jnp.float32
BF16 = jnp.bfloat16
I32 = jnp.int32

EPS = 1e-6
HEAD_DIM = 128
N_HEADS = 8
N_GROUPS = 8
GROUP_DIM = 128
CHUNK = 128
IDX_HEADS = 16
IDX_DIM = 64
TOPK_MAX = 256
PAGE_SIZE = 128
N_MOD = 9
SUBLANES = 8
PAGE_ROWS = PAGE_SIZE * N_HEADS

INT_MIN = -(2 ** 31)
INT_MAX = 2 ** 31 - 1
F32_MIN = float(jnp.finfo(jnp.float32).min)
LOG2E = math.log2(math.e)

VMEM_LIMIT_BYTES = 60000 * 1024

NT_DIMS = (((1,), (1,)), ((), ()))


def _compiler_params(semantics):
    return pltpu.CompilerParams(dimension_semantics=semantics, vmem_limit_bytes=VMEM_LIMIT_BYTES)


def _modulate(x, gain, shift, scale):
    ms = jnp.mean(x * x, axis=-1, keepdims=True)
    y = x * lax.rsqrt(ms + EPS)
    return (y * gain) * (1.0 + scale) + shift


def _rms_rows(x, gain):
    ms = jnp.mean(x * x, axis=-1, keepdims=True)
    return (x * lax.rsqrt(ms + EPS)) * gain


def _div_pow2(x, n):
    assert n & (n - 1) == 0
    return lax.shift_right_logical(x, jnp.int32(n.bit_length() - 1))


def _mod_pow2(x, n):
    assert n & (n - 1) == 0
    return x & jnp.int32(n - 1)


LANES = 128


def _fold_lane_tiles(x, op):
    return functools.reduce(op, [x[:, i * LANES:(i + 1) * LANES] for i in range(x.shape[1] // LANES)])


def _rms_cols(x, gain_col):
    ms = jnp.mean(x * x, axis=0, keepdims=True)
    return (x * lax.rsqrt(ms + EPS)) * gain_col


def _ada_kernel(c_ref, w_ref, b_ref, o_ref):
    a = jax.nn.silu(c_ref[...]).astype(BF16)
    o_ref[...] = jnp.dot(a, w_ref[...].astype(BF16), preferred_element_type=F32) + b_ref[...]


def _ada_call(c_all, w_ada, b_ada):
    rows, d = c_all.shape
    n = w_ada.shape[1]
    tn = 1024
    return pl.pallas_call(
        _ada_kernel,
        grid=(n // tn,),
        in_specs=[
            pl.BlockSpec((rows, d), lambda j: (0, 0)),
            pl.BlockSpec((d, tn), lambda j: (0, j)),
            pl.BlockSpec((1, tn), lambda j: (0, j)),
        ],
        out_specs=pl.BlockSpec((rows, tn), lambda j: (0, j)),
        out_shape=jax.ShapeDtypeStruct((rows, n), F32),
        compiler_params=_compiler_params(("parallel",)),
        name="ada_proj",
    )(c_all, w_ada, b_ada.reshape(1, n))


def _ffn_kernel(x_ref, sh_ref, sc_ref, g_ref, nw_ref, wg_ref, wu_ref, wd_ref, o_ref, *rest, emit_bf16):
    if emit_bf16:
        wgb_ref, wub_ref, wdb_ref, xn_ref = rest
    else:
        (xn_ref,) = rest
    n = pl.program_id(1)

    @pl.when(n == 0)
    def _():
        xn_ref[...] = _modulate(x_ref[...], nw_ref[...], sh_ref[...], sc_ref[...]).astype(BF16)
        o_ref[...] = jnp.zeros_like(o_ref)

    wg, wu, wd = wg_ref[...].astype(BF16), wu_ref[...].astype(BF16), wd_ref[...].astype(BF16)
    if emit_bf16:
        wgb_ref[...] = wg
        wub_ref[...] = wu
        wdb_ref[...] = wd
    xn = xn_ref[...]
    gate = jnp.dot(xn, wg, preferred_element_type=F32)
    up = jnp.dot(xn, wu, preferred_element_type=F32)
    hmid = (jax.nn.silu(gate) * up).astype(BF16)
    o_ref[...] += jnp.dot(hmid, wd, preferred_element_type=F32)

    @pl.when(n == pl.num_programs(1) - 1)
    def _():
        o_ref[...] = x_ref[...] + (0.5 * g_ref[...]) * o_ref[...]


def _mod_spec(mod, tiles_per_group):
    _, r, d = mod.shape
    return pl.BlockSpec((None, r, d), lambda i, *_: (i // tiles_per_group, 0, 0))


def _ffn_call(x, shift, scale, gate, norm_w, wg, wu, wd, *, tm, tiles_per_group, emit_bf16=False):
    m, d = x.shape
    dff = wg.shape[1]
    tn = 512
    assert not emit_bf16 or m == tm
    col = lambda: pl.BlockSpec((d, tn), lambda i, n: (0, n))
    row = lambda: pl.BlockSpec((tn, d), lambda i, n: (n, 0))
    out_specs = [pl.BlockSpec((tm, d), lambda i, n: (i, 0))]
    out_shape = [jax.ShapeDtypeStruct((m, d), F32)]
    if emit_bf16:
        out_specs += [col(), col(), row()]
        out_shape += [jax.ShapeDtypeStruct(w.shape, BF16) for w in (wg, wu, wd)]
    outs = pl.pallas_call(
        functools.partial(_ffn_kernel, emit_bf16=emit_bf16),
        grid=(m // tm, dff // tn),
        in_specs=[
            pl.BlockSpec((tm, d), lambda i, n: (i, 0), pipeline_mode=pl.Buffered(1)),
            _mod_spec(shift, tiles_per_group),
            _mod_spec(scale, tiles_per_group),
            _mod_spec(gate, tiles_per_group),
            pl.BlockSpec((1, d), lambda i, n: (0, 0)),
            col(), col(), row(),
        ],
        out_specs=out_specs,
        out_shape=out_shape,
        scratch_shapes=[pltpu.VMEM((tm, d), BF16)],
        compiler_params=_compiler_params(("parallel", "arbitrary")),
        name="ffn",
    )(x, shift, scale, gate, norm_w.reshape(1, d), wg, wu, wd)
    return outs if emit_bf16 else outs[0]


W_ATTN = N_HEADS * HEAD_DIM
W_IDX = IDX_HEADS * IDX_DIM
W_GATE = N_GROUPS * GROUP_DIM
OFF_Q, OFF_K, OFF_V = 0, W_ATTN, 2 * W_ATTN
OFF_QI = 3 * W_ATTN
OFF_KIW = OFF_QI + W_IDX
OFF_U = OFF_KIW + IDX_DIM + IDX_HEADS
OFF_VB = OFF_U + W_GATE
D_IN = OFF_VB + W_GATE
W_IN_PADDED = -(-D_IN // LANES) * LANES


def _inproj_kernel(h_ref, sh_ref, sc_ref, nw_ref, w_ref, qn_ref, kn_ref, ikn_ref, vn_ref, *out_refs, decode):
    if decode:
        q_ref, k_ref, kb_ref, v_ref, u_ref, vb_ref = out_refs
    else:
        qT_ref, qiT_ref, wT_ref, vT_ref, k_ref, kb_ref, v_ref, ki_ref, kib_ref, u_ref, vb_ref = out_refs
    a = _modulate(h_ref[...], nw_ref[...], sh_ref[...], sc_ref[...]).astype(BF16)

    def proj(off, width):
        return jnp.dot(a, w_ref[:, off:off + width], preferred_element_type=F32)

    tm = h_ref.shape[0]
    head_rows = lambda h: pl.ds(h, tm, stride=N_HEADS)

    q = proj(OFF_Q, W_ATTN)
    k = proj(OFF_K, W_ATTN)
    for h in range(N_HEADS):
        cols = slice(h * HEAD_DIM, (h + 1) * HEAD_DIM)
        qh = _rms_rows(q[:, cols], qn_ref[...])
        if decode:
            q_ref[:, cols] = qh.astype(BF16)
        else:
            qT_ref[cols, :] = qh.T.astype(BF16)
        kh = _rms_rows(k[:, cols], kn_ref[...])
        kb_ref[:, cols] = kh.astype(BF16)
        k_ref[head_rows(h), :] = kh
    v = proj(OFF_V, W_ATTN)
    for h in range(N_HEADS):
        v_ref[head_rows(h), :] = v[:, h * HEAD_DIM:(h + 1) * HEAD_DIM]
    def proj_window(off, width):
        start = off // LANES * LANES
        win = proj(start, -(-(off + width - start) // LANES) * LANES)
        return win, win[:, off - start:off - start + width]

    u_win, u = proj_window(OFF_U, W_GATE)
    u_ref[...] = jax.nn.gelu(u)
    vb = jax.nn.gelu(proj_window(OFF_VB, W_GATE)[1])
    for g in range(N_GROUPS):
        cols = slice(g * GROUP_DIM, (g + 1) * GROUP_DIM)
        vb_ref[:, cols] = _rms_rows(vb[:, cols], vn_ref[g:g + 1, :])
    if not decode:
        vT_ref[...] = v.T.astype(BF16)
        qiT_ref[...] = (proj(OFF_QI, W_IDX) * (IDX_DIM ** -0.5)).T.astype(BF16)
        kiw = u_win[:, 0:LANES]
        ki = _rms_rows(kiw[:, 0:IDX_DIM], ikn_ref[...])
        ki_ref[...] = ki
        kib_ref[...] = ki.astype(BF16)
        wT_ref[...] = kiw.T[IDX_DIM:IDX_DIM + IDX_HEADS, :] * (IDX_HEADS ** -0.5)


def _inproj_call(h, shift, scale, norm_w, w_packed, q_norm, k_norm, idx_k_norm, v_norm, *, tm, tiles_per_group,
                 decode):
    m, d = h.shape
    nt = m // tm
    row = lambda i: (i, 0)
    fm = lambda i: (i, 0, 0)
    const = lambda i: (0, 0)
    token_major = lambda width, dtype: (jax.ShapeDtypeStruct((m, width), dtype), pl.BlockSpec((tm, width), row))
    feature_major = lambda width, dtype: (jax.ShapeDtypeStruct((nt, width, tm), dtype),
                                          pl.BlockSpec((None, width, tm), fm))
    if decode:
        outs = [token_major(W_ATTN, BF16)]
    else:
        outs = [feature_major(W_ATTN, BF16), feature_major(W_IDX, BF16),
                feature_major(IDX_HEADS, F32), feature_major(W_ATTN, BF16)]
    head_major = lambda: (jax.ShapeDtypeStruct((m * N_HEADS, HEAD_DIM), F32),
                          pl.BlockSpec((tm * N_HEADS, HEAD_DIM), row))
    outs += [head_major(), token_major(W_ATTN, BF16), head_major()]
    if not decode:
        outs += [token_major(IDX_DIM, F32), token_major(IDX_DIM, BF16)]
    outs += [token_major(W_GATE, F32), token_major(W_GATE, F32)]
    return pl.pallas_call(
        functools.partial(_inproj_kernel, decode=decode),
        grid=(nt,),
        in_specs=[
            pl.BlockSpec((tm, d), row),
            _mod_spec(shift, tiles_per_group),
            _mod_spec(scale, tiles_per_group),
            pl.BlockSpec((1, d), const),
            pl.BlockSpec(w_packed.shape, const),
            pl.BlockSpec((1, HEAD_DIM), const),
            pl.BlockSpec((1, HEAD_DIM), const),
            pl.BlockSpec((1, IDX_DIM), const),
            pl.BlockSpec((N_GROUPS, GROUP_DIM), const),
        ],
        out_specs=[spec for _, spec in outs],
        out_shape=[shape for shape, _ in outs],
        compiler_params=_compiler_params(("parallel",)),
        name="in_proj",
    )(h, shift, scale, norm_w.reshape(1, d), w_packed, q_norm.reshape(1, HEAD_DIM), k_norm.reshape(1, HEAD_DIM),
      idx_k_norm.reshape(1, IDX_DIM), v_norm)


def _dot3(a, b, dims=(((1,), (0,)), ((), ()))):
    a_hi = a.astype(BF16)
    b_hi = b.astype(BF16)
    a_lo = (a - a_hi.astype(F32)).astype(BF16)
    b_lo = (b - b_hi.astype(F32)).astype(BF16)
    dot = functools.partial(lax.dot_general, dimension_numbers=dims, preferred_element_type=F32)
    return dot(a_hi, b_hi) + (dot(a_lo, b_hi) + dot(a_hi, b_lo))


def _idxproj_kernel(h_ref, sh_ref, sc_ref, nw_ref, w_ref, ikn_ref, qi_ref, ki_ref, wi_ref):
    a = _modulate(h_ref[...], nw_ref[...], sh_ref[...], sc_ref[...])
    r = _dot3(a, w_ref[...])
    qi_ref[...] = r[:, 0:W_IDX] * (IDX_DIM ** -0.5)
    kiw = r[:, W_IDX:W_IDX + LANES]
    ki_ref[...] = _rms_rows(kiw[:, 0:IDX_DIM], ikn_ref[...])
    wi_ref[...] = kiw[:, IDX_DIM:IDX_DIM + IDX_HEADS] * (IDX_HEADS ** -0.5)


def _idxproj_call(h, shift, scale, norm_w, w_idx, idx_k_norm):
    m, d = h.shape
    full = lambda a: pl.BlockSpec(a.shape, lambda i: (0,) * a.ndim)
    args = (h, shift[0], scale[0], norm_w.reshape(1, d), w_idx, idx_k_norm.reshape(1, IDX_DIM))
    return pl.pallas_call(
        _idxproj_kernel,
        grid=(1,),
        in_specs=[full(a) for a in args],
        out_specs=[pl.BlockSpec((m, W_IDX), lambda i: (0, 0)), pl.BlockSpec((m, IDX_DIM), lambda i: (0, 0)),
                   pl.BlockSpec((m, IDX_HEADS), lambda i: (0, 0))],
        out_shape=[jax.ShapeDtypeStruct((m, W_IDX), F32), jax.ShapeDtypeStruct((m, IDX_DIM), F32),
                   jax.ShapeDtypeStruct((m, IDX_HEADS), F32)],
        compiler_params=_compiler_params(("arbitrary",)),
        name="idx_proj_decode",
    )(*args)


def _key_to_float(key):
    return lax.bitcast_convert_type(jnp.where(key < 0, key ^ jnp.int32(INT_MAX), key), F32)


def _topk_threshold(count_ge, topk, like):
    zero = jnp.zeros_like(like)
    cnt0 = count_ge(_key_to_float(zero))
    nonneg = cnt0 >= topk
    t0 = jnp.where(nonneg, zero, jnp.int32(INT_MIN))
    c0 = jnp.where(nonneg, cnt0, jnp.int32(topk))

    def body(p, carry):
        t, ct = carry
        cand = t | jnp.left_shift(jnp.int32(1), 31 - p)
        cnt = count_ge(_key_to_float(cand))
        take = cnt >= topk
        return jnp.where(take, cand, t), jnp.where(take, cnt, ct)

    t, ct = lax.fori_loop(1, 32, body, (t0, c0))
    return t != jnp.int32(INT_MIN), _key_to_float(t), ct


def _tie_cutoff(count_where, thr, topk, nbits, excess):
    need = topk - count_where(lambda sc, pos: sc > thr)
    j = jnp.zeros_like(need)
    for bit in reversed(range(nbits)):
        cand = j | jnp.int32(1 << bit)
        cnt = count_where(lambda sc, pos, cand=cand: (sc == thr) & (pos < cand))
        j = jnp.where(cnt < need, cand, j)
    return jnp.where(excess, j, jnp.int32(INT_MAX))


def _select(count_where, topk, pos_bits, like):
    has_thr, thr, cnt_thr = _topk_threshold(lambda t: count_where(lambda sc, pos: sc >= t), topk, like)
    excess = (cnt_thr > topk) & has_thr
    cutoff = lax.cond(
        jnp.max(excess.astype(I32)) > 0,
        lambda: _tie_cutoff(count_where, thr, topk, pos_bits, excess),
        lambda: jnp.full(like.shape, INT_MAX, I32),
    )
    return lambda sc, pos: jnp.logical_not(has_thr) | (sc > thr) | ((sc == thr) & (pos <= cutoff))


def _attn_prompt_kernel(qT_ref, qiT_ref, wT_ref, kb_ref, vT_ref, kib_ref, o_ref,
                        sc_ref, dm_ref, acc_ref, m_ref, l_ref, a_ref, s_ref, p_ref, *, tq, topk, pos_bits):
    j = pl.program_id(1)
    nchunks = j + 1
    groups = tq // SUBLANES
    t_pos = j * tq + lax.broadcasted_iota(I32, (1, tq), 1)
    s_off = lax.broadcasted_iota(I32, (tq, 1), 0)

    def chunk_start(c):
        return pl.multiple_of(c * tq, tq)

    def score_body(c, carry):
        off = chunk_start(c)
        kic = kib_ref[pl.ds(off, tq), :]
        sc = jnp.zeros((tq, tq), F32)
        for h in range(IDX_HEADS):
            d = jnp.dot(kic, qiT_ref[h * IDX_DIM:(h + 1) * IDX_DIM, :], preferred_element_type=F32)
            sc = sc + jnp.maximum(d, 0.0) * wT_ref[h:h + 1, :]
        sc_ref[pl.ds(off, tq), :] = jnp.where((off + s_off) <= t_pos, sc, -jnp.inf)
        return carry

    lax.fori_loop(0, nchunks, score_body, 0)

    def count_where(pred):
        def body(c, cnt):
            off = chunk_start(c)
            hit = pred(sc_ref[pl.ds(off, tq), :], off + s_off).reshape(groups, SUBLANES, tq)
            accs = [cnt] + [jnp.zeros_like(cnt)] * 3
            for g in range(groups):
                accs[g % 4] = jnp.where(hit[g], accs[g % 4] + 1, accs[g % 4])
            return (accs[0] + accs[1]) + (accs[2] + accs[3])

        cnt8 = lax.fori_loop(0, nchunks, body, jnp.zeros((SUBLANES, tq), I32))
        return jnp.sum(cnt8, axis=0, keepdims=True)

    selected = _select(count_where, topk, pos_bits, t_pos)

    def dm_body(c, carry):
        off = chunk_start(c)
        pos = off + s_off
        sel = selected(sc_ref[pl.ds(off, tq), :], pos) & (pos <= t_pos)
        dm_ref[pl.ds(off, tq), :] = jnp.where(sel, (t_pos - pos).astype(F32) * LOG2E, jnp.inf)
        return carry

    lax.fori_loop(0, nchunks, dm_body, 0)

    m_ref[...] = jnp.full(m_ref.shape, F32_MIN, F32)
    l_ref[...] = jnp.zeros_like(l_ref)
    acc_ref[...] = jnp.zeros_like(acc_ref)
    qk_scale = (HEAD_DIM ** -0.5) * LOG2E

    def att_body(c, carry):
        off = chunk_start(c)
        dmc = dm_ref[pl.ds(off, tq), :]
        for h in range(N_HEADS):
            rows = slice(h * HEAD_DIM, (h + 1) * HEAD_DIM)
            qk = jnp.dot(kb_ref[pl.ds(off, tq), rows], qT_ref[rows, :], preferred_element_type=F32)
            s = qk * qk_scale - (2.0 ** -(h + 1)) * dmc
            s_ref[h] = s
            cm = jnp.max(s.reshape(groups, SUBLANES, tq), axis=0)
            m_old = m_ref[h]
            m_new = jnp.maximum(m_old, jnp.broadcast_to(jnp.max(cm, axis=0, keepdims=True), cm.shape))
            a_ref[h] = jnp.exp2(m_old - m_new)
            m_ref[h] = m_new
        for h in range(N_HEADS):
            p = jnp.exp2(s_ref[h].reshape(groups, SUBLANES, tq) - m_ref[h][None])
            l_ref[h] = a_ref[h] * l_ref[h] + jnp.sum(p, axis=0)
            p_ref[h] = p.reshape(tq, tq).astype(BF16)
        for h in range(N_HEADS):
            rows = slice(h * HEAD_DIM, (h + 1) * HEAD_DIM)
            pv = jnp.dot(vT_ref[c, rows, :], p_ref[h], preferred_element_type=F32)
            acc = acc_ref[rows, :].reshape(HEAD_DIM // SUBLANES, SUBLANES, tq) * a_ref[h][None]
            acc_ref[rows, :] = acc.reshape(HEAD_DIM, tq) + pv
        return carry

    lax.fori_loop(0, nchunks, att_body, 0)

    for h in range(N_HEADS):
        rows = slice(h * HEAD_DIM, (h + 1) * HEAD_DIM)
        acc_ref[rows, :] = acc_ref[rows, :] / jnp.sum(l_ref[h], axis=0, keepdims=True)
    o_ref[...] = acc_ref[...].T


def _attn_prompt_call(qT, qiT, wT, kb, vT, kib, *, batch, seq, tq):
    w_attn = N_HEADS * HEAD_DIM
    w_idx = IDX_HEADS * IDX_DIM
    nq = seq // tq
    topk = min(TOPK_MAX, seq // 4)
    qblk = lambda b, j: (b * nq + j, 0, 0)
    kernel = functools.partial(_attn_prompt_kernel, tq=tq, topk=topk, pos_bits=max(1, (seq - 1).bit_length()))
    return pl.pallas_call(
        kernel,
        grid=(batch, nq),
        in_specs=[
            pl.BlockSpec((None, w_attn, tq), qblk),
            pl.BlockSpec((None, w_idx, tq), qblk),
            pl.BlockSpec((None, IDX_HEADS, tq), qblk),
            pl.BlockSpec((seq, w_attn), lambda b, j: (b, 0), pipeline_mode=pl.Buffered(1)),
            pl.BlockSpec((nq, w_attn, tq), lambda b, j: (b, 0, 0), pipeline_mode=pl.Buffered(1)),
            pl.BlockSpec((seq, IDX_DIM), lambda b, j: (b, 0), pipeline_mode=pl.Buffered(1)),
        ],
        out_specs=pl.BlockSpec((tq, w_attn), lambda b, j: (b * nq + j, 0)),
        out_shape=jax.ShapeDtypeStruct((batch * seq, w_attn), F32),
        scratch_shapes=[
            pltpu.VMEM((seq, tq), F32),
            pltpu.VMEM((seq, tq), F32),
            pltpu.VMEM((w_attn, tq), F32),
            pltpu.VMEM((N_HEADS, SUBLANES, tq), F32),
            pltpu.VMEM((N_HEADS, SUBLANES, tq), F32),
            pltpu.VMEM((N_HEADS, SUBLANES, tq), F32),
            pltpu.VMEM((N_HEADS, tq, tq), F32),
            pltpu.VMEM((N_HEADS, tq, tq), BF16),
        ],
        compiler_params=_compiler_params(("parallel", "arbitrary")),
        name="attn_prompt",
    )(qT, qiT, wT, kb, vT, kib)


def _outproj_kernel(oa_ref, u_ref, vb_ref, h_ref, g_ref, ws_ref, bs_ref, na_ref, nb_ref, wo_ref, o_ref, *, unit):
    tm = oa_ref.shape[0]
    w_attn = N_HEADS * HEAD_DIM
    rows_i = lax.broadcasted_iota(I32, (unit, unit), 0)
    cols_i = lax.broadcasted_iota(I32, (unit, unit), 1)
    causal = rows_i >= cols_i
    ob_parts = []
    for g in range(N_GROUPS):
        cols = slice(g * GROUP_DIM, (g + 1) * GROUP_DIM)
        wm = jnp.where(causal, ws_ref[g], 0.0).astype(BF16)
        bias = bs_ref[:, g:g + 1]
        mixed = []
        for r in range(tm // unit):
            rws = slice(r * unit, (r + 1) * unit)
            mixed.append(jnp.dot(wm, vb_ref[rws, cols].astype(BF16), preferred_element_type=F32) + bias)
        mixed = mixed[0] if len(mixed) == 1 else jnp.concatenate(mixed, axis=0)
        ob_parts.append(u_ref[:, cols] * mixed)
    ob = jnp.concatenate(ob_parts, axis=1)
    za = _rms_rows(oa_ref[...], na_ref[...]).astype(BF16)
    zb = _rms_rows(ob, nb_ref[...]).astype(BF16)
    mix = (jnp.dot(za, wo_ref[0:w_attn, :], preferred_element_type=F32)
           + jnp.dot(zb, wo_ref[w_attn:, :], preferred_element_type=F32))
    o_ref[...] = h_ref[...] + g_ref[...] * mix


def _outproj_call(out_a, u, vb, h, gate, ws, bs_t, norm_a, norm_b, w_out, *, tm, tiles_per_group, unit):
    m, d = h.shape
    w_attn = out_a.shape[1]
    w_g = u.shape[1]
    row = lambda i: (i, 0)
    const = lambda i: (0, 0)
    return pl.pallas_call(
        functools.partial(_outproj_kernel, unit=unit),
        grid=(m // tm,),
        in_specs=[
            pl.BlockSpec((tm, w_attn), row),
            pl.BlockSpec((tm, w_g), row),
            pl.BlockSpec((tm, w_g), row),
            pl.BlockSpec((tm, d), row),
            _mod_spec(gate, tiles_per_group),
            pl.BlockSpec(ws.shape, lambda i: (0, 0, 0)),
            pl.BlockSpec(bs_t.shape, const),
            pl.BlockSpec((1, w_attn), const),
            pl.BlockSpec((1, w_g), const),
            pl.BlockSpec(w_out.shape, const),
        ],
        out_specs=pl.BlockSpec((tm, d), row),
        out_shape=jax.ShapeDtypeStruct((m, d), F32),
        compiler_params=_compiler_params(("parallel",)),
        name="out_proj",
    )(out_a, u, vb, h, gate, ws, bs_t, norm_a.reshape(1, w_attn), norm_b.reshape(1, w_g), w_out)


def _select_decode_kernel(pt_ref, qi_ref, w_ref, kin_ref, *rest, pages_per_step, n_pages, topk, pos_bits):
    page_refs = rest[:pages_per_step]
    bias_ref = rest[pages_per_step]
    sc_ref = rest[pages_per_step + 1]
    p = pl.program_id(1)
    qi = qi_ref[...]
    w = w_ref[...]

    def page_scores(keys):
        d = _dot3(qi, keys, NT_DIMS)
        r = jnp.maximum(d, 0.0) * w
        return jnp.sum(r.reshape(IDX_HEADS, SUBLANES, PAGE_SIZE), axis=0)

    for i in range(pages_per_step):
        sc_ref[p * pages_per_step + i] = page_scores(page_refs[i][...])

    @pl.when(p == pl.num_programs(1) - 1)
    def _():
        shape = (n_pages + 1, SUBLANES, PAGE_SIZE)
        tok = lax.broadcasted_iota(I32, (SUBLANES, PAGE_SIZE), 0)
        lane = lax.broadcasted_iota(I32, (SUBLANES, PAGE_SIZE), 1)
        sc_ref[n_pages] = jnp.where(lane <= tok, page_scores(kin_ref[...]), -jnp.inf)
        pos = lax.broadcasted_iota(I32, shape, 0) * PAGE_SIZE + lax.broadcasted_iota(I32, shape, 2)
        q_pos = n_pages * PAGE_SIZE + lax.broadcasted_iota(I32, shape, 1)

        def count_where(pred):
            hit = pred(sc_ref[...], pos).astype(I32)
            return jnp.sum(jnp.sum(hit, axis=0), axis=-1, keepdims=True)

        selected = _select(count_where, topk, pos_bits, jnp.zeros((SUBLANES, 1), I32))
        sel = selected(sc_ref[...], pos) & (pos <= q_pos)
        sel2d = jnp.where(sel, 1.0, 0.0).astype(BF16).reshape((n_pages + 1) * SUBLANES, PAGE_SIZE)
        expand = (_div_pow2(lax.broadcasted_iota(I32, (PAGE_SIZE, PAGE_ROWS), 1), N_HEADS)
                  == lax.broadcasted_iota(I32, (PAGE_SIZE, PAGE_ROWS), 0))
        rep = jnp.dot(sel2d, jnp.where(expand, 1.0, 0.0).astype(BF16), preferred_element_type=F32)
        bias_ref[...] = jnp.where(rep > 0.5, 0.0, -jnp.inf).reshape(n_pages + 1, SUBLANES, PAGE_ROWS)


def _select_decode_call(page_table, qi2d, wcol, ki_new, cache_kidx, *, pages_per_step):
    nseq, n_pages = page_table.shape
    topk = min(TOPK_MAX, (n_pages * PAGE_SIZE + SUBLANES) // 4)
    pos_bits = max(1, ((n_pages + 1) * PAGE_SIZE - 1).bit_length())
    per_seq = lambda b, p, pt: (b, 0, 0)

    def page_spec(i):
        return pl.BlockSpec((None, PAGE_SIZE, IDX_DIM),
                            lambda b, p, pt, i=i: (pt[b, p * pages_per_step + i], 0, 0))

    kernel = functools.partial(_select_decode_kernel, pages_per_step=pages_per_step, n_pages=n_pages,
                               topk=topk, pos_bits=pos_bits)
    grid_spec = pltpu.PrefetchScalarGridSpec(
        num_scalar_prefetch=1,
        grid=(nseq, n_pages // pages_per_step),
        in_specs=[
            pl.BlockSpec((None, IDX_HEADS * SUBLANES, IDX_DIM), per_seq),
            pl.BlockSpec((None, IDX_HEADS * SUBLANES, 1), per_seq),
            pl.BlockSpec((None, PAGE_SIZE, IDX_DIM), per_seq),
        ] + [page_spec(i) for i in range(pages_per_step)],
        out_specs=pl.BlockSpec((None, n_pages + 1, SUBLANES, PAGE_ROWS), lambda b, p, pt: (b, 0, 0, 0)),
        scratch_shapes=[pltpu.VMEM((n_pages + 1, SUBLANES, PAGE_SIZE), F32)],
    )
    return pl.pallas_call(
        kernel,
        grid_spec=grid_spec,
        out_shape=jax.ShapeDtypeStruct((nseq, n_pages + 1, SUBLANES, PAGE_ROWS), F32),
        compiler_params=_compiler_params(("parallel", "arbitrary")),
        name="select_decode",
    )(page_table, qi2d, wcol, ki_new, *([cache_kidx] * pages_per_step))


def _attn_decode_kernel(pt_ref, q_ref, slope_ref, bias_ref, kn_ref, vn_ref, *rest, pages_per_step, n_pages):
    k_refs = rest[:pages_per_step]
    v_refs = rest[pages_per_step:2 * pages_per_step]
    o_ref, a_ref, acc_ref, m_ref, l_ref = rest[2 * pages_per_step:]
    p = pl.program_id(1)
    nrow = N_HEADS * SUBLANES
    qk_scale = (HEAD_DIM ** -0.5) * LOG2E
    slope = slope_ref[...] * LOG2E

    @pl.when(p == 0)
    def _():
        m_ref[...] = jnp.full(m_ref.shape, F32_MIN, F32)
        l_ref[...] = jnp.zeros_like(l_ref)
        acc_ref[...] = jnp.zeros_like(acc_ref)
        row = lax.broadcasted_iota(I32, (nrow, PAGE_ROWS), 0)
        col = lax.broadcasted_iota(I32, (nrow, PAGE_ROWS), 1)
        same_head = _mod_pow2(col, N_HEADS) == _div_pow2(row, SUBLANES)
        tok_minus_key = (_mod_pow2(row, SUBLANES) - _div_pow2(col, N_HEADS)).astype(F32)
        a_ref[...] = jnp.where(same_head, -slope * tok_minus_key, -jnp.inf)

    def logits(keys_bf16, page):
        page_dist = jnp.full((nrow, 1), (n_pages - page) * PAGE_SIZE, I32).astype(F32)
        s = lax.dot_general(q_ref[...], keys_bf16, NT_DIMS, preferred_element_type=F32)
        return s * qk_scale + a_ref[...] - slope * page_dist + jnp.concatenate([bias_ref[page]] * N_HEADS, axis=0)

    def update(s_pages, v_pages):
        lanes_max = functools.reduce(jnp.maximum, [_fold_lane_tiles(s, jnp.maximum) for s in s_pages])
        m_old = m_ref[...]
        m_new = jnp.maximum(m_old, jnp.max(lanes_max, axis=-1, keepdims=True))
        alpha = jnp.exp2(m_old - m_new)
        lanes_sum = jnp.zeros((nrow, PAGE_SIZE), F32)
        acc = alpha * acc_ref[...]
        for s, vals in zip(s_pages, v_pages):
            pr = jnp.exp2(s - m_new)
            lanes_sum = lanes_sum + _fold_lane_tiles(pr, jnp.add)
            acc = acc + jnp.dot(pr.astype(BF16), vals, preferred_element_type=F32)
        l_ref[...] = alpha * l_ref[...] + jnp.sum(lanes_sum, axis=-1, keepdims=True)
        acc_ref[...] = acc
        m_ref[...] = m_new

    update([logits(k_refs[i][...].astype(BF16), p * pages_per_step + i) for i in range(pages_per_step)],
           [v_refs[i][...].astype(BF16) for i in range(pages_per_step)])

    @pl.when(p == pl.num_programs(1) - 1)
    def _():
        update([logits(kn_ref[...], n_pages)], [vn_ref[...]])
        o_ref[...] = acc_ref[...] / l_ref[...]


def _attn_decode_call(page_table, q_rows, slope_col, bias, k_new, v_new, cache_k, cache_v, *, pages_per_step):
    nseq, n_pages = page_table.shape
    nrow = N_HEADS * SUBLANES
    per_seq = lambda b, p, pt: (b, 0, 0)

    def page_spec(i):
        return pl.BlockSpec((None, PAGE_ROWS, HEAD_DIM),
                            lambda b, p, pt, i=i: (pt[b, p * pages_per_step + i], 0, 0))

    kernel = functools.partial(_attn_decode_kernel, pages_per_step=pages_per_step, n_pages=n_pages)
    grid_spec = pltpu.PrefetchScalarGridSpec(
        num_scalar_prefetch=1,
        grid=(nseq, n_pages // pages_per_step),
        in_specs=[
            pl.BlockSpec((None, nrow, HEAD_DIM), per_seq),
            pl.BlockSpec((nrow, 1), lambda b, p, pt: (0, 0)),
            pl.BlockSpec((None, n_pages + 1, SUBLANES, PAGE_ROWS), lambda b, p, pt: (b, 0, 0, 0)),
            pl.BlockSpec((None, PAGE_ROWS, HEAD_DIM), per_seq),
            pl.BlockSpec((None, PAGE_ROWS, HEAD_DIM), per_seq),
        ] + [page_spec(i) for i in range(pages_per_step)] + [page_spec(i) for i in range(pages_per_step)],
        out_specs=pl.BlockSpec((None, nrow, HEAD_DIM), per_seq),
        scratch_shapes=[
            pltpu.VMEM((nrow, PAGE_ROWS), F32),
            pltpu.VMEM((nrow, HEAD_DIM), F32),
            pltpu.VMEM((nrow, 1), F32),
            pltpu.VMEM((nrow, 1), F32),
        ],
    )
    return pl.pallas_call(
        kernel,
        grid_spec=grid_spec,
        out_shape=jax.ShapeDtypeStruct((nseq, nrow, HEAD_DIM), F32),
        compiler_params=_compiler_params(("parallel", "arbitrary")),
        name="attn_decode",
    )(page_table, q_rows, slope_col, bias, k_new, v_new,
      *([cache_k] * pages_per_step), *([cache_v] * pages_per_step))


def _pack_in_weights(w_in):
    assert w_in.shape[1] == D_IN and OFF_U // LANES * LANES == OFF_KIW
    w_padded = jnp.pad(w_in.astype(BF16), ((0, 0), (0, W_IN_PADDED - D_IN)))
    w_idx = jnp.pad(w_in[:, OFF_QI:OFF_U], ((0, 0), (0, OFF_KIW + LANES - OFF_U)))
    return w_padded, w_idx


def _alibi_slopes():
    return jnp.exp2(-8.0 * jnp.arange(1, N_HEADS + 1, dtype=F32) / N_HEADS)


def _forward(x_prompt, x_sample, cache_k, cache_v, cache_kidx, page_table, c_prompt, c_sample,
             w_ada, b_ada, norm_ffn1, w1_gate, w1_up, w1_down, norm_mix, w_in, q_norm, k_norm,
             idx_k_norm, w_spatial, b_spatial, v_norm, out_norm_a, out_norm_b, w_out, norm_ffn2,
             w2_gate, w2_up, w2_down):
    batch, seq, d = x_prompt.shape
    nseq, ntok, _ = x_sample.shape
    n_pages = page_table.shape[1]
    n_pool = cache_k.shape[1]
    w_attn = N_HEADS * HEAD_DIM
    w_g = N_GROUPS * GROUP_DIM
    assert ntok == SUBLANES and seq % 256 == 0 and cache_k.shape[0] == 1
    tq = 256
    tm_ffn = next(t for t in (1024, 512, 256) if seq % t == 0)
    m_s = nseq * ntok

    w_packed, w_idx = _pack_in_weights(w_in[0])
    wo = w_out[0].astype(BF16)

    n_c = batch + nseq
    c_all = jnp.concatenate([c_prompt, c_sample, jnp.zeros((-n_c % 8, d), F32)], axis=0)
    mods = _ada_call(c_all, w_ada[0], b_ada[0])
    mods_p = [mods[:batch, i * d:(i + 1) * d].reshape(batch, 1, d) for i in range(N_MOD)]
    mods_s = [jnp.repeat(mods[batch:n_c, i * d:(i + 1) * d], ntok, axis=0).reshape(1, m_s, d) for i in range(N_MOD)]

    def stream(x, mod, w1, tm_f, tm_p, tiles_f, tiles_p, decode):
        sh1, sc1, g1, sh2, sc2, g2, sh3, sc3, g3 = mod
        h = _ffn_call(x, sh1, sc1, g1, norm_ffn1[0], *w1, tm=tm_f, tiles_per_group=tiles_f, emit_bf16=decode)
        if decode:
            h, *w1 = h
        proj = _inproj_call(h, sh2, sc2, norm_mix[0], w_packed, q_norm[0], k_norm[0],
                            idx_k_norm[0], v_norm[0], tm=tm_p, tiles_per_group=tiles_p, decode=decode)
        return h, proj, (g2, sh3, sc3, g3), w1

    def finish(h, out_a, u, vb, rest, w2, ws, bs_t, tm_f, tm_p, tiles_f, tiles_p, unit, decode):
        g2, sh3, sc3, g3 = rest
        h2 = _outproj_call(out_a, u, vb, h, g2, ws, bs_t, out_norm_a[0], out_norm_b[0], wo,
                           tm=tm_p, tiles_per_group=tiles_p, unit=unit)
        return _ffn_call(h2, sh3, sc3, g3, norm_ffn2[0], *w2, tm=tm_f, tiles_per_group=tiles_f,
                         emit_bf16=decode)

    xs = x_sample.reshape(m_s, d)
    h_s, proj_s, rest_s, w1_bf16 = stream(xs, mods_s, (w1_gate[0], w1_up[0], w1_down[0]), m_s, m_s, 1, 1, True)
    q_s, k_s, kb_s, v_s, u_s, vb_s = proj_s
    qi_s, ki_s, wi_s = _idxproj_call(h_s, mods_s[3], mods_s[4], norm_mix[0], w_idx, idx_k_norm[0])
    q_rows = q_s.reshape(nseq, ntok, N_HEADS, HEAD_DIM).transpose(0, 2, 1, 3).reshape(
        nseq, N_HEADS * ntok, HEAD_DIM)
    qi2d = qi_s.reshape(nseq, ntok, IDX_HEADS, IDX_DIM).transpose(0, 2, 1, 3).reshape(
        nseq, IDX_HEADS * ntok, IDX_DIM)
    wcol = wi_s.reshape(nseq, ntok, IDX_HEADS).transpose(0, 2, 1).reshape(nseq, IDX_HEADS * ntok, 1)

    def pad_rows(a, rows):
        a = a.reshape(nseq, -1, a.shape[-1])
        return jnp.pad(a, ((0, 0), (0, rows - a.shape[1]), (0, 0)))

    bias = _select_decode_call(page_table, qi2d, wcol, pad_rows(ki_s, PAGE_SIZE),
                               cache_kidx[0], pages_per_step=math.gcd(n_pages, 16))
    slope_col = jnp.repeat(_alibi_slopes(), ntok).reshape(N_HEADS * ntok, 1)
    out_a_s = _attn_decode_call(page_table, q_rows, slope_col, bias,
                                pad_rows(kb_s.reshape(m_s * N_HEADS, HEAD_DIM), PAGE_ROWS),
                                pad_rows(v_s.astype(BF16).reshape(m_s * N_HEADS, HEAD_DIM), PAGE_ROWS),
                                cache_k.reshape(n_pool, PAGE_ROWS, HEAD_DIM),
                                cache_v.reshape(n_pool, PAGE_ROWS, HEAD_DIM),
                                pages_per_step=math.gcd(n_pages, 8))
    out_a_s = out_a_s.reshape(nseq, N_HEADS, ntok, HEAD_DIM).transpose(0, 2, 1, 3).reshape(m_s, w_attn)
    ws_s = jnp.einsum("ab,gts->gatbs", jnp.eye(nseq, dtype=F32), w_spatial[0][:, :ntok, :ntok]).reshape(
        N_GROUPS, m_s, m_s)
    bs_s = jnp.tile(b_spatial[0][:, :ntok].T, (nseq, 1))
    y_s, *w2_bf16 = finish(h_s, out_a_s, u_s, vb_s, rest_s, (w2_gate[0], w2_up[0], w2_down[0]), ws_s, bs_s,
                           m_s, m_s, 1, 1, m_s, True)

    xp = x_prompt.reshape(batch * seq, d)
    tf_p, tp_p = seq // tm_ffn, seq // tq
    h_p, proj_p, rest_p, _ = stream(xp, mods_p, w1_bf16, tm_ffn, tq, tf_p, tp_p, False)
    qT, qiT, wT, vT, k_p, kb_p, v_p, ki_p, kib_p, u_p, vb_p = proj_p
    out_a_p = _attn_prompt_call(qT, qiT, wT, kb_p, vT, kib_p, batch=batch, seq=seq, tq=tq)
    y_p = finish(h_p, out_a_p, u_p, vb_p, rest_p, w2_bf16, w_spatial[0], b_spatial[0].T,
                 tm_ffn, tq, tf_p, tp_p, CHUNK, False)

    return (
        y_p.reshape(batch, seq, d),
        y_s.reshape(nseq, ntok, d),
        k_p.reshape(1, batch, seq, N_HEADS, HEAD_DIM),
        v_p.reshape(1, batch, seq, N_HEADS, HEAD_DIM),
        ki_p.reshape(1, batch, seq, IDX_DIM),
        k_s.reshape(1, nseq, ntok, N_HEADS, HEAD_DIM),
        v_s.reshape(1, nseq, ntok, N_HEADS, HEAD_DIM),
        ki_s.reshape(1, nseq, ntok, IDX_DIM),
        vb_s.reshape(1, nseq, ntok, w_g),
    )


def kernel(x_prompt, x_sample, cache_k, cache_v, cache_kidx, page_table, c_prompt, c_sample, w_ada, b_ada, norm_ffn1, w1_gate, w1_up, w1_down, norm_mix, w_in, q_norm, k_norm, idx_k_norm, w_spatial, b_spatial, v_norm, out_norm_a, out_norm_b, w_out, norm_ffn2, w2_gate, w2_up, w2_down):
    return _forward(x_prompt, x_sample, cache_k, cache_v, cache_kidx, page_table, c_prompt, c_sample,
                    w_ada, b_ada, norm_ffn1, w1_gate, w1_up, w1_down, norm_mix, w_in, q_norm, k_norm,
                    idx_k_norm, w_spatial, b_spatial, v_norm, out_norm_a, out_norm_b, w_out, norm_ffn2,
                    w2_gate, w2_up, w2_down)
```

```python
import functools
import math

import jax
import jax.numpy as jnp
from jax import lax
from jax.experimental import pallas as pl
from jax.experimental.pallas import tpu as pltpu

F32 = jnp.float32
BF16 = jnp.bfloat16
I32 = jnp.int32

EPS = 1e-6
HEAD_DIM = 128
N_HEADS = 8
N_GROUPS = 8
GROUP_DIM = 128
CHUNK = 128
IDX_HEADS = 16
IDX_DIM = 64
TOPK_MAX = 256
PAGE_SIZE = 128
N_MOD = 9
SUBLANES = 8
PAGE_ROWS = PAGE_SIZE * N_HEADS

INT_MIN = -(2 ** 31)
INT_MAX = 2 ** 31 - 1
F32_MIN = float(jnp.finfo(jnp.float32).min)
LOG2E = math.log2(math.e)

VMEM_LIMIT_BYTES = 60000 * 1024

NT_DIMS = (((1,), (1,)), ((), ()))


def _compiler_params(semantics):
    return pltpu.CompilerParams(dimension_semantics=semantics, vmem_limit_bytes=VMEM_LIMIT_BYTES)


def _modulate(x, gain, shift, scale):
    ms = jnp.mean(x * x, axis=-1, keepdims=True)
    y = x * lax.rsqrt(ms + EPS)
    return (y * gain) * (1.0 + scale) + shift


def _rms_rows(x, gain):
    ms = jnp.mean(x * x, axis=-1, keepdims=True)
    return (x * lax.rsqrt(ms + EPS)) * gain


def _div_pow2(x, n):
    assert n & (n - 1) == 0
    return lax.shift_right_logical(x, jnp.int32(n.bit_length() - 1))


def _mod_pow2(x, n):
    assert n & (n - 1) == 0
    return x & jnp.int32(n - 1)


LANES = 128


def _fold_lane_tiles(x, op):
    return functools.reduce(op, [x[:, i * LANES:(i + 1) * LANES] for i in range(x.shape[1] // LANES)])


def _rms_cols(x, gain_col):
    ms = jnp.mean(x * x, axis=0, keepdims=True)
    return (x * lax.rsqrt(ms + EPS)) * gain_col


def _ada_kernel(c_ref, w_ref, b_ref, o_ref):
    a = jax.nn.silu(c_ref[...]).astype(BF16)
    o_ref[...] = jnp.dot(a, w_ref[...].astype(BF16), preferred_element_type=F32) + b_ref[...]


def _ada_call(c_all, w_ada, b_ada):
    rows, d = c_all.shape
    n = w_ada.shape[1]
    tn = 1024
    return pl.pallas_call(
        _ada_kernel,
        grid=(n // tn,),
        in_specs=[
            pl.BlockSpec((rows, d), lambda j: (0, 0)),
            pl.BlockSpec((d, tn), lambda j: (0, j)),
            pl.BlockSpec((1, tn), lambda j: (0, j)),
        ],
        out_specs=pl.BlockSpec((rows, tn), lambda j: (0, j)),
        out_shape=jax.ShapeDtypeStruct((rows, n), F32),
        compiler_params=_compiler_params(("parallel",)),
        name="ada_proj",
    )(c_all, w_ada, b_ada.reshape(1, n))


def _ffn_kernel(x_ref, sh_ref, sc_ref, g_ref, nw_ref, wg_ref, wu_ref, wd_ref, o_ref, *rest, emit_bf16):
    if emit_bf16:
        wgb_ref, wub_ref, wdb_ref, xn_ref = rest
    else:
        (xn_ref,) = rest
    n = pl.program_id(1)

    @pl.when(n == 0)
    def _():
        xn_ref[...] = _modulate(x_ref[...], nw_ref[...], sh_ref[...], sc_ref[...]).astype(BF16)
        o_ref[...] = jnp.zeros_like(o_ref)

    wg, wu, wd = wg_ref[...].astype(BF16), wu_ref[...].astype(BF16), wd_ref[...].astype(BF16)
    if emit_bf16:
        wgb_ref[...] = wg
        wub_ref[...] = wu
        wdb_ref[...] = wd
    xn = xn_ref[...]
    gate = jnp.dot(xn, wg, preferred_element_type=F32)
    up = jnp.dot(xn, wu, preferred_element_type=F32)
    hmid = (jax.nn.silu(gate) * up).astype(BF16)
    o_ref[...] += jnp.dot(hmid, wd, preferred_element_type=F32)

    @pl.when(n == pl.num_programs(1) - 1)
    def _():
        o_ref[...] = x_ref[...] + (0.5 * g_ref[...]) * o_ref[...]


def _mod_spec(mod, tiles_per_group):
    _, r, d = mod.shape
    return pl.BlockSpec((None, r, d), lambda i, *_: (i // tiles_per_group, 0, 0))


def _ffn_call(x, shift, scale, gate, norm_w, wg, wu, wd, *, tm, tiles_per_group, emit_bf16=False):
    m, d = x.shape
    dff = wg.shape[1]
    tn = 512
    assert not emit_bf16 or m == tm
    col = lambda: pl.BlockSpec((d, tn), lambda i, n: (0, n))
    row = lambda: pl.BlockSpec((tn, d), lambda i, n: (n, 0))
    out_specs = [pl.BlockSpec((tm, d), lambda i, n: (i, 0))]
    out_shape = [jax.ShapeDtypeStruct((m, d), F32)]
    if emit_bf16:
        out_specs += [col(), col(), row()]
        out_shape += [jax.ShapeDtypeStruct(w.shape, BF16) for w in (wg, wu, wd)]
    outs = pl.pallas_call(
        functools.partial(_ffn_kernel, emit_bf16=emit_bf16),
        grid=(m // tm, dff // tn),
        in_specs=[
            pl.BlockSpec((tm, d), lambda i, n: (i, 0), pipeline_mode=pl.Buffered(1)),
            _mod_spec(shift, tiles_per_group),
            _mod_spec(scale, tiles_per_group),
            _mod_spec(gate, tiles_per_group),
            pl.BlockSpec((1, d), lambda i, n: (0, 0)),
            col(), col(), row(),
        ],
        out_specs=out_specs,
        out_shape=out_shape,
        scratch_shapes=[pltpu.VMEM((tm, d), BF16)],
        compiler_params=_compiler_params(("parallel", "arbitrary")),
        name="ffn",
    )(x, shift, scale, gate, norm_w.reshape(1, d), wg, wu, wd)
    return outs if emit_bf16 else outs[0]


W_ATTN = N_HEADS * HEAD_DIM
W_IDX = IDX_HEADS * IDX_DIM
W_GATE = N_GROUPS * GROUP_DIM
OFF_Q, OFF_K, OFF_V = 0, W_ATTN, 2 * W_ATTN
OFF_QI = 3 * W_ATTN
OFF_KIW = OFF_QI + W_IDX
OFF_U = OFF_KIW + IDX_DIM + IDX_HEADS
OFF_VB = OFF_U + W_GATE
D_IN = OFF_VB + W_GATE
assert OFF_U % 16 == 0 and OFF_VB % 16 == 0


def _head_rows(h, tm):
    return pl.ds(h, tm, stride=N_HEADS)


def _inproj_kernel(h_ref, sh_ref, sc_ref, nw_ref, w_ref, qn_ref, kn_ref, ikn_ref, vn_ref,
                   qT_ref, qiT_ref, wT_ref, vT_ref, k_ref, kb_ref, v_ref, ki_ref, kib_ref, u_ref, vb_ref):
    tm = h_ref.shape[0]
    a = _modulate(h_ref[...], nw_ref[...], sh_ref[...], sc_ref[...]).astype(BF16)

    def proj(off, rows):
        return lax.dot_general(w_ref[off:off + rows, :], a, NT_DIMS, preferred_element_type=F32)

    q_t = proj(OFF_Q, W_ATTN)
    k_t = proj(OFF_K, W_ATTN)
    v_t = proj(OFF_V, W_ATTN)
    for h in range(N_HEADS):
        rows = slice(h * HEAD_DIM, (h + 1) * HEAD_DIM)
        qT_ref[rows, :] = _rms_cols(q_t[rows, :], qn_ref[...]).astype(BF16)
        kh = _rms_cols(k_t[rows, :], kn_ref[...]).T
        kb_ref[:, rows] = kh.astype(BF16)
        k_ref[_head_rows(h, tm), :] = kh
        v_ref[_head_rows(h, tm), :] = v_t[rows, :].T
    vT_ref[...] = v_t.astype(BF16)
    qiT_ref[...] = (proj(OFF_QI, W_IDX) * (IDX_DIM ** -0.5)).astype(BF16)
    tail = proj(OFF_KIW, LANES)
    wT_ref[...] = tail[IDX_DIM:IDX_DIM + IDX_HEADS, :] * (IDX_HEADS ** -0.5)
    ki_t = _rms_cols(tail[0:IDX_DIM, :], ikn_ref[...])
    ki = jnp.concatenate([ki_t, tail[IDX_DIM:, :]], axis=0).T[:, 0:IDX_DIM]
    ki_ref[...] = ki
    kib_ref[...] = ki.astype(BF16)
    u_t = proj(OFF_U, W_GATE)
    vb_t = proj(OFF_VB, W_GATE)
    for g in range(N_GROUPS):
        rows = slice(g * GROUP_DIM, (g + 1) * GROUP_DIM)
        u_ref[:, rows] = jax.nn.gelu(u_t[rows, :]).T
        vb_ref[:, rows] = _rms_cols(jax.nn.gelu(vb_t[rows, :]), vn_ref[:, g:g + 1]).T


def _inproj_decode_kernel(h_ref, sh_ref, sc_ref, nw_ref, w_ref, qn_ref, kn_ref, vn_ref,
                          q_ref, k_ref, kb_ref, v_ref, u_ref, vb_ref):
    tm = h_ref.shape[0]
    a = _modulate(h_ref[...], nw_ref[...], sh_ref[...], sc_ref[...]).astype(BF16)

    def proj(off, width):
        return lax.dot_general(a, w_ref[off:off + width, :], NT_DIMS, preferred_element_type=F32)

    q = proj(OFF_Q, W_ATTN)
    k = proj(OFF_K, W_ATTN)
    v = proj(OFF_V, W_ATTN)
    for h in range(N_HEADS):
        cols = slice(h * HEAD_DIM, (h + 1) * HEAD_DIM)
        q_ref[:, cols] = _rms_rows(q[:, cols], qn_ref[...]).astype(BF16)
        kh = _rms_rows(k[:, cols], kn_ref[...])
        kb_ref[:, cols] = kh.astype(BF16)
        k_ref[_head_rows(h, tm), :] = kh
        v_ref[_head_rows(h, tm), :] = v[:, cols]
    u_ref[...] = jax.nn.gelu(proj(OFF_U, W_GATE))
    vb = jax.nn.gelu(proj(OFF_VB, W_GATE))
    for g in range(N_GROUPS):
        cols = slice(g * GROUP_DIM, (g + 1) * GROUP_DIM)
        vb_ref[:, cols] = _rms_rows(vb[:, cols], vn_ref[g:g + 1, :])


def _inproj_call(h, shift, scale, norm_w, w_t, q_norm, k_norm, idx_k_norm, v_norm, *, tm, tiles_per_group,
                 decode):
    m, d = h.shape
    nt = m // tm
    row = lambda i: (i, 0)
    fm = lambda i: (i, 0, 0)
    token_major = lambda width, dtype: (jax.ShapeDtypeStruct((m, width), dtype), pl.BlockSpec((tm, width), row))
    feature_major = lambda width, dtype: (jax.ShapeDtypeStruct((nt, width, tm), dtype),
                                          pl.BlockSpec((None, width, tm), fm))
    head_major = lambda: (jax.ShapeDtypeStruct((m * N_HEADS, HEAD_DIM), F32),
                          pl.BlockSpec((tm * N_HEADS, HEAD_DIM), row))
    if decode:
        outs = [token_major(W_ATTN, BF16)]
        gains = [q_norm.reshape(1, HEAD_DIM), k_norm.reshape(1, HEAD_DIM), v_norm]
    else:
        outs = [feature_major(W_ATTN, BF16), feature_major(W_IDX, BF16),
                feature_major(IDX_HEADS, F32), feature_major(W_ATTN, BF16)]
        gains = [q_norm.reshape(HEAD_DIM, 1), k_norm.reshape(HEAD_DIM, 1), idx_k_norm.reshape(IDX_DIM, 1),
                 v_norm.T]
    outs += [head_major(), token_major(W_ATTN, BF16), head_major()]
    if not decode:
        outs += [token_major(IDX_DIM, F32), token_major(IDX_DIM, BF16)]
    outs += [token_major(W_GATE, F32), token_major(W_GATE, F32)]
    const2d = lambda a: pl.BlockSpec(a.shape, lambda i: (0, 0))
    return pl.pallas_call(
        _inproj_decode_kernel if decode else _inproj_kernel,
        grid=(nt,),
        in_specs=[
            pl.BlockSpec((tm, d), row),
            _mod_spec(shift, tiles_per_group),
            _mod_spec(scale, tiles_per_group),
            pl.BlockSpec((1, d), lambda i: (0, 0)),
            const2d(w_t),
        ] + [const2d(g) for g in gains],
        out_specs=[spec for _, spec in outs],
        out_shape=[shape for shape, _ in outs],
        compiler_params=_compiler_params(("parallel",)),
        name="in_proj",
    )(h, shift, scale, norm_w.reshape(1, d), w_t, *gains)


def _dot3(a, b, dims=(((1,), (0,)), ((), ()))):
    a_hi = a.astype(BF16)
    b_hi = b.astype(BF16)
    a_lo = (a - a_hi.astype(F32)).astype(BF16)
    b_lo = (b - b_hi.astype(F32)).astype(BF16)
    dot = functools.partial(lax.dot_general, dimension_numbers=dims, preferred_element_type=F32)
    return dot(a_hi, b_hi) + (dot(a_lo, b_hi) + dot(a_hi, b_lo))


def _idxproj_kernel(h_ref, sh_ref, sc_ref, nw_ref, w_ref, ikn_ref, qi_ref, ki_ref, wi_ref):
    a = _modulate(h_ref[...], nw_ref[...], sh_ref[...], sc_ref[...])
    r = _dot3(a, w_ref[...], NT_DIMS)
    qi_ref[...] = r[:, 0:W_IDX] * (IDX_DIM ** -0.5)
    kiw = r[:, W_IDX:W_IDX + LANES]
    ki_ref[...] = _rms_rows(kiw[:, 0:IDX_DIM], ikn_ref[...])
    wi_ref[...] = kiw[:, IDX_DIM:IDX_DIM + IDX_HEADS] * (IDX_HEADS ** -0.5)


def _idxproj_call(h, shift, scale, norm_w, w_idx, idx_k_norm):
    m, d = h.shape
    full = lambda a: pl.BlockSpec(a.shape, lambda i: (0,) * a.ndim)
    args = (h, shift[0], scale[0], norm_w.reshape(1, d), w_idx, idx_k_norm.reshape(1, IDX_DIM))
    return pl.pallas_call(
        _idxproj_kernel,
        grid=(1,),
        in_specs=[full(a) for a in args],
        out_specs=[pl.BlockSpec((m, W_IDX), lambda i: (0, 0)), pl.BlockSpec((m, IDX_DIM), lambda i: (0, 0)),
                   pl.BlockSpec((m, IDX_HEADS), lambda i: (0, 0))],
        out_shape=[jax.ShapeDtypeStruct((m, W_IDX), F32), jax.ShapeDtypeStruct((m, IDX_DIM), F32),
                   jax.ShapeDtypeStruct((m, IDX_HEADS), F32)],
        compiler_params=_compiler_params(("arbitrary",)),
        name="idx_proj_decode",
    )(*args)


def _key_to_float(key):
    return lax.bitcast_convert_type(jnp.where(key < 0, key ^ jnp.int32(INT_MAX), key), F32)


def _topk_threshold(count_ge, topk, like):
    zero = jnp.zeros_like(like)
    cnt0 = count_ge(_key_to_float(zero))
    nonneg = cnt0 >= topk
    t0 = jnp.where(nonneg, zero, jnp.int32(INT_MIN))
    c0 = jnp.where(nonneg, cnt0, jnp.int32(topk))

    def body(p, carry):
        t, ct = carry
        cand = t | jnp.left_shift(jnp.int32(1), 31 - p)
        cnt = count_ge(_key_to_float(cand))
        take = cnt >= topk
        return jnp.where(take, cand, t), jnp.where(take, cnt, ct)

    t, ct = lax.fori_loop(1, 32, body, (t0, c0))
    return t != jnp.int32(INT_MIN), _key_to_float(t), ct


def _tie_cutoff(count_where, thr, topk, nbits, excess):
    need = topk - count_where(lambda sc, pos: sc > thr)
    j = jnp.zeros_like(need)
    for bit in reversed(range(nbits)):
        cand = j | jnp.int32(1 << bit)
        cnt = count_where(lambda sc, pos, cand=cand: (sc == thr) & (pos < cand))
        j = jnp.where(cnt < need, cand, j)
    return jnp.where(excess, j, jnp.int32(INT_MAX))


def _select(count_where, topk, pos_bits, like):
    has_thr, thr, cnt_thr = _topk_threshold(lambda t: count_where(lambda sc, pos: sc >= t), topk, like)
    excess = (cnt_thr > topk) & has_thr
    cutoff = lax.cond(
        jnp.max(excess.astype(I32)) > 0,
        lambda: _tie_cutoff(count_where, thr, topk, pos_bits, excess),
        lambda: jnp.full(like.shape, INT_MAX, I32),
    )
    return lambda sc, pos: jnp.logical_not(has_thr) | (sc > thr) | ((sc == thr) & (pos <= cutoff))


def _attn_prompt_kernel(qT_ref, qiT_ref, wT_ref, kb_ref, vT_ref, kib_ref, o_ref,
                        sc_ref, dm_ref, acc_ref, m_ref, l_ref, a_ref, s_ref, p_ref, *, tq, topk, pos_bits):
    j = pl.program_id(1)
    nchunks = j + 1
    groups = tq // SUBLANES
    t_pos = j * tq + lax.broadcasted_iota(I32, (1, tq), 1)
    s_off = lax.broadcasted_iota(I32, (tq, 1), 0)

    def chunk_start(c):
        return pl.multiple_of(c * tq, tq)

    def score_body(c, carry):
        off = chunk_start(c)
        kic = kib_ref[pl.ds(off, tq), :]
        sc = jnp.zeros((tq, tq), F32)
        for h in range(IDX_HEADS):
            d = jnp.dot(kic, qiT_ref[h * IDX_DIM:(h + 1) * IDX_DIM, :], preferred_element_type=F32)
            sc = sc + jnp.maximum(d, 0.0) * wT_ref[h:h + 1, :]
        sc_ref[pl.ds(off, tq), :] = jnp.where((off + s_off) <= t_pos, sc, -jnp.inf)
        return carry

    lax.fori_loop(0, nchunks, score_body, 0)

    def count_where(pred):
        def body(c, cnt):
            off = chunk_start(c)
            hit = pred(sc_ref[pl.ds(off, tq), :], off + s_off).reshape(groups, SUBLANES, tq)
            accs = [cnt] + [jnp.zeros_like(cnt)] * 3
            for g in range(groups):
                accs[g % 4] = jnp.where(hit[g], accs[g % 4] + 1, accs[g % 4])
            return (accs[0] + accs[1]) + (accs[2] + accs[3])

        cnt8 = lax.fori_loop(0, nchunks, body, jnp.zeros((SUBLANES, tq), I32))
        return jnp.sum(cnt8, axis=0, keepdims=True)

    selected = _select(count_where, topk, pos_bits, t_pos)

    def dm_body(c, carry):
        off = chunk_start(c)
        pos = off + s_off
        sel = selected(sc_ref[pl.ds(off, tq), :], pos) & (pos <= t_pos)
        dm_ref[pl.ds(off, tq), :] = jnp.where(sel, (t_pos - pos).astype(F32) * LOG2E, jnp.inf)
        return carry

    lax.fori_loop(0, nchunks, dm_body, 0)

    m_ref[...] = jnp.full(m_ref.shape, F32_MIN, F32)
    l_ref[...] = jnp.zeros_like(l_ref)
    acc_ref[...] = jnp.zeros_like(acc_ref)
    qk_scale = (HEAD_DIM ** -0.5) * LOG2E

    def att_body(c, carry):
        off = chunk_start(c)
        dmc = dm_ref[pl.ds(off, tq), :]
        for h in range(N_HEADS):
            rows = slice(h * HEAD_DIM, (h + 1) * HEAD_DIM)
            qk = jnp.dot(kb_ref[pl.ds(off, tq), rows], qT_ref[rows, :], preferred_element_type=F32)
            s = qk * qk_scale - (2.0 ** -(h + 1)) * dmc
            s_ref[h] = s
            cm = jnp.max(s.reshape(groups, SUBLANES, tq), axis=0)
            m_old = m_ref[h]
            m_new = jnp.maximum(m_old, jnp.broadcast_to(jnp.max(cm, axis=0, keepdims=True), cm.shape))
            a_ref[h] = jnp.exp2(m_old - m_new)
            m_ref[h] = m_new
        for h in range(N_HEADS):
            p = jnp.exp2(s_ref[h].reshape(groups, SUBLANES, tq) - m_ref[h][None])
            l_ref[h] = a_ref[h] * l_ref[h] + jnp.sum(p, axis=0)
            p_ref[h] = p.reshape(tq, tq).astype(BF16)
        for h in range(N_HEADS):
            rows = slice(h * HEAD_DIM, (h + 1) * HEAD_DIM)
            pv = jnp.dot(vT_ref[c, rows, :], p_ref[h], preferred_element_type=F32)
            acc = acc_ref[rows, :].reshape(HEAD_DIM // SUBLANES, SUBLANES, tq) * a_ref[h][None]
            acc_ref[rows, :] = acc.reshape(HEAD_DIM, tq) + pv
        return carry

    lax.fori_loop(0, nchunks, att_body, 0)

    for h in range(N_HEADS):
        rows = slice(h * HEAD_DIM, (h + 1) * HEAD_DIM)
        acc_ref[rows, :] = acc_ref[rows, :] / jnp.sum(l_ref[h], axis=0, keepdims=True)
    o_ref[...] = acc_ref[...].T


def _attn_prompt_call(qT, qiT, wT, kb, vT, kib, *, batch, seq, tq):
    w_attn = N_HEADS * HEAD_DIM
    w_idx = IDX_HEADS * IDX_DIM
    nq = seq // tq
    topk = min(TOPK_MAX, seq // 4)
    qblk = lambda b, j: (b * nq + j, 0, 0)
    kernel = functools.partial(_attn_prompt_kernel, tq=tq, topk=topk, pos_bits=max(1, (seq - 1).bit_length()))
    return pl.pallas_call(
        kernel,
        grid=(batch, nq),
        in_specs=[
            pl.BlockSpec((None, w_attn, tq), qblk),
            pl.BlockSpec((None, w_idx, tq), qblk),
            pl.BlockSpec((None, IDX_HEADS, tq), qblk),
            pl.BlockSpec((seq, w_attn), lambda b, j: (b, 0), pipeline_mode=pl.Buffered(1)),
            pl.BlockSpec((nq, w_attn, tq), lambda b, j: (b, 0, 0), pipeline_mode=pl.Buffered(1)),
            pl.BlockSpec((seq, IDX_DIM), lambda b, j: (b, 0), pipeline_mode=pl.Buffered(1)),
        ],
        out_specs=pl.BlockSpec((tq, w_attn), lambda b, j: (b * nq + j, 0)),
        out_shape=jax.ShapeDtypeStruct((batch * seq, w_attn), F32),
        scratch_shapes=[
            pltpu.VMEM((seq, tq), F32),
            pltpu.VMEM((seq, tq), F32),
            pltpu.VMEM((w_attn, tq), F32),
            pltpu.VMEM((N_HEADS, SUBLANES, tq), F32),
            pltpu.VMEM((N_HEADS, SUBLANES, tq), F32),
            pltpu.VMEM((N_HEADS, SUBLANES, tq), F32),
            pltpu.VMEM((N_HEADS, tq, tq), F32),
            pltpu.VMEM((N_HEADS, tq, tq), BF16),
        ],
        compiler_params=_compiler_params(("parallel", "arbitrary")),
        name="attn_prompt",
    )(qT, qiT, wT, kb, vT, kib)


def _outproj_kernel(oa_ref, u_ref, vb_ref, h_ref, g_ref, ws_ref, bs_ref, na_ref, nb_ref, wo_ref, o_ref, *, unit):
    tm = oa_ref.shape[0]
    w_attn = N_HEADS * HEAD_DIM
    rows_i = lax.broadcasted_iota(I32, (unit, unit), 0)
    cols_i = lax.broadcasted_iota(I32, (unit, unit), 1)
    causal = rows_i >= cols_i
    ob_parts = []
    for g in range(N_GROUPS):
        cols = slice(g * GROUP_DIM, (g + 1) * GROUP_DIM)
        wm = jnp.where(causal, ws_ref[g], 0.0).astype(BF16)
        bias = bs_ref[:, g:g + 1]
        mixed = []
        for r in range(tm // unit):
            rws = slice(r * unit, (r + 1) * unit)
            mixed.append(jnp.dot(wm, vb_ref[rws, cols].astype(BF16), preferred_element_type=F32) + bias)
        mixed = mixed[0] if len(mixed) == 1 else jnp.concatenate(mixed, axis=0)
        ob_parts.append(u_ref[:, cols] * mixed)
    ob = jnp.concatenate(ob_parts, axis=1)
    za = _rms_rows(oa_ref[...], na_ref[...]).astype(BF16)
    zb = _rms_rows(ob, nb_ref[...]).astype(BF16)
    mix = (jnp.dot(za, wo_ref[0:w_attn, :], preferred_element_type=F32)
           + jnp.dot(zb, wo_ref[w_attn:, :], preferred_element_type=F32))
    o_ref[...] = h_ref[...] + g_ref[...] * mix


def _outproj_call(out_a, u, vb, h, gate, ws, bs_t, norm_a, norm_b, w_out, *, tm, tiles_per_group, unit):
    m, d = h.shape
    w_attn = out_a.shape[1]
    w_g = u.shape[1]
    row = lambda i: (i, 0)
    const = lambda i: (0, 0)
    return pl.pallas_call(
        functools.partial(_outproj_kernel, unit=unit),
        grid=(m // tm,),
        in_specs=[
            pl.BlockSpec((tm, w_attn), row),
            pl.BlockSpec((tm, w_g), row),
            pl.BlockSpec((tm, w_g), row),
            pl.BlockSpec((tm, d), row),
            _mod_spec(gate, tiles_per_group),
            pl.BlockSpec(ws.shape, lambda i: (0, 0, 0)),
            pl.BlockSpec(bs_t.shape, const),
            pl.BlockSpec((1, w_attn), const),
            pl.BlockSpec((1, w_g), const),
            pl.BlockSpec(w_out.shape, const),
        ],
        out_specs=pl.BlockSpec((tm, d), row),
        out_shape=jax.ShapeDtypeStruct((m, d), F32),
        compiler_params=_compiler_params(("parallel",)),
        name="out_proj",
    )(out_a, u, vb, h, gate, ws, bs_t, norm_a.reshape(1, w_attn), norm_b.reshape(1, w_g), w_out)


def _select_decode_kernel(pt_ref, qi_ref, w_ref, kin_ref, *rest, pages_per_step, n_pages, topk, pos_bits):
    page_refs = rest[:pages_per_step]
    bias_ref = rest[pages_per_step]
    sc_ref = rest[pages_per_step + 1]
    p = pl.program_id(1)
    qi = qi_ref[...]
    w = w_ref[...]

    def page_scores(keys_t):
        d = _dot3(qi, keys_t)
        r = jnp.maximum(d, 0.0) * w
        return jnp.sum(r.reshape(IDX_HEADS, SUBLANES, PAGE_SIZE), axis=0)

    for i in range(pages_per_step):
        sc_ref[p * pages_per_step + i] = page_scores(page_refs[i][...])

    @pl.when(p == pl.num_programs(1) - 1)
    def _():
        shape = (n_pages + 1, SUBLANES, PAGE_SIZE)
        tok = lax.broadcasted_iota(I32, (SUBLANES, PAGE_SIZE), 0)
        lane = lax.broadcasted_iota(I32, (SUBLANES, PAGE_SIZE), 1)
        sc_ref[n_pages] = jnp.where(lane <= tok, page_scores(kin_ref[...]), -jnp.inf)
        pos = lax.broadcasted_iota(I32, shape, 0) * PAGE_SIZE + lax.broadcasted_iota(I32, shape, 2)
        q_pos = n_pages * PAGE_SIZE + lax.broadcasted_iota(I32, shape, 1)

        def count_where(pred):
            hit = pred(sc_ref[...], pos).astype(I32)
            return jnp.sum(jnp.sum(hit, axis=0), axis=-1, keepdims=True)

        selected = _select(count_where, topk, pos_bits, jnp.zeros((SUBLANES, 1), I32))
        sel = selected(sc_ref[...], pos) & (pos <= q_pos)
        sel2d = jnp.where(sel, 1.0, 0.0).astype(BF16).reshape((n_pages + 1) * SUBLANES, PAGE_SIZE)
        expand = (_div_pow2(lax.broadcasted_iota(I32, (PAGE_SIZE, PAGE_ROWS), 1), N_HEADS)
                  == lax.broadcasted_iota(I32, (PAGE_SIZE, PAGE_ROWS), 0))
        rep = jnp.dot(sel2d, jnp.where(expand, 1.0, 0.0).astype(BF16), preferred_element_type=F32)
        bias_ref[...] = jnp.where(rep > 0.5, 0.0, -jnp.inf).reshape(n_pages + 1, SUBLANES, PAGE_ROWS)


def _select_decode_call(page_table, qi2d, wcol, ki_new, cache_kidx, *, pages_per_step):
    nseq, n_pages = page_table.shape
    topk = min(TOPK_MAX, (n_pages * PAGE_SIZE + SUBLANES) // 4)
    pos_bits = max(1, ((n_pages + 1) * PAGE_SIZE - 1).bit_length())
    per_seq = lambda b, p, pt: (b, 0, 0)

    def page_spec(i):
        return pl.BlockSpec((None, IDX_DIM, PAGE_SIZE),
                            lambda b, p, pt, i=i: (pt[b, p * pages_per_step + i], 0, 0))

    kernel = functools.partial(_select_decode_kernel, pages_per_step=pages_per_step, n_pages=n_pages,
                               topk=topk, pos_bits=pos_bits)
    grid_spec = pltpu.PrefetchScalarGridSpec(
        num_scalar_prefetch=1,
        grid=(nseq, n_pages // pages_per_step),
        in_specs=[
            pl.BlockSpec((None, IDX_HEADS * SUBLANES, IDX_DIM), per_seq),
            pl.BlockSpec((None, IDX_HEADS * SUBLANES, 1), per_seq),
            pl.BlockSpec((None, IDX_DIM, PAGE_SIZE), per_seq),
        ] + [page_spec(i) for i in range(pages_per_step)],
        out_specs=pl.BlockSpec((None, n_pages + 1, SUBLANES, PAGE_ROWS), lambda b, p, pt: (b, 0, 0, 0)),
        scratch_shapes=[pltpu.VMEM((n_pages + 1, SUBLANES, PAGE_SIZE), F32)],
    )
    return pl.pallas_call(
        kernel,
        grid_spec=grid_spec,
        out_shape=jax.ShapeDtypeStruct((nseq, n_pages + 1, SUBLANES, PAGE_ROWS), F32),
        compiler_params=_compiler_params(("parallel", "arbitrary")),
        name="select_decode",
    )(page_table, qi2d, wcol, ki_new, *([cache_kidx] * pages_per_step))


def _attn_decode_kernel(pt_ref, q_ref, slope_ref, bias_ref, kn_ref, vn_ref, *rest, pages_per_step, n_pages):
    k_refs = rest[:pages_per_step]
    v_refs = rest[pages_per_step:2 * pages_per_step]
    o_ref, a_ref, acc_ref, m_ref, l_ref = rest[2 * pages_per_step:]
    p = pl.program_id(1)
    nrow = N_HEADS * SUBLANES
    qk_scale = (HEAD_DIM ** -0.5) * LOG2E
    slope = slope_ref[...] * LOG2E

    @pl.when(p == 0)
    def _():
        m_ref[...] = jnp.full(m_ref.shape, F32_MIN, F32)
        l_ref[...] = jnp.zeros_like(l_ref)
        acc_ref[...] = jnp.zeros_like(acc_ref)
        row = lax.broadcasted_iota(I32, (nrow, PAGE_ROWS), 0)
        col = lax.broadcasted_iota(I32, (nrow, PAGE_ROWS), 1)
        same_head = _mod_pow2(col, N_HEADS) == _div_pow2(row, SUBLANES)
        tok_minus_key = (_mod_pow2(row, SUBLANES) - _div_pow2(col, N_HEADS)).astype(F32)
        a_ref[...] = jnp.where(same_head, -slope * tok_minus_key, -jnp.inf)

    def logits(keys_bf16, page):
        page_dist = jnp.full((nrow, 1), (n_pages - page) * PAGE_SIZE, I32).astype(F32)
        s = lax.dot_general(q_ref[...], keys_bf16, NT_DIMS, preferred_element_type=F32)
        return s * qk_scale + a_ref[...] - slope * page_dist + jnp.concatenate([bias_ref[page]] * N_HEADS, axis=0)

    def update(s_pages, v_pages):
        lanes_max = functools.reduce(jnp.maximum, [_fold_lane_tiles(s, jnp.maximum) for s in s_pages])
        m_old = m_ref[...]
        m_new = jnp.maximum(m_old, jnp.max(lanes_max, axis=-1, keepdims=True))
        alpha = jnp.exp2(m_old - m_new)
        lanes_sum = jnp.zeros((nrow, PAGE_SIZE), F32)
        acc = alpha * acc_ref[...]
        for s, vals in zip(s_pages, v_pages):
            pr = jnp.exp2(s - m_new)
            lanes_sum = lanes_sum + _fold_lane_tiles(pr, jnp.add)
            acc = acc + jnp.dot(pr.astype(BF16), vals, preferred_element_type=F32)
        l_ref[...] = alpha * l_ref[...] + jnp.sum(lanes_sum, axis=-1, keepdims=True)
        acc_ref[...] = acc
        m_ref[...] = m_new

    update([logits(k_refs[i][...].astype(BF16), p * pages_per_step + i) for i in range(pages_per_step)],
           [v_refs[i][...].astype(BF16) for i in range(pages_per_step)])

    @pl.when(p == pl.num_programs(1) - 1)
    def _():
        update([logits(kn_ref[...], n_pages)], [vn_ref[...]])
        o_ref[...] = acc_ref[...] / l_ref[...]


def _attn_decode_call(page_table, q_rows, slope_col, bias, k_new, v_new, cache_k, cache_v, *, pages_per_step):
    nseq, n_pages = page_table.shape
    nrow = N_HEADS * SUBLANES
    per_seq = lambda b, p, pt: (b, 0, 0)

    def page_spec(i):
        return pl.BlockSpec((None, PAGE_ROWS, HEAD_DIM),
                            lambda b, p, pt, i=i: (pt[b, p * pages_per_step + i], 0, 0))

    kernel = functools.partial(_attn_decode_kernel, pages_per_step=pages_per_step, n_pages=n_pages)
    grid_spec = pltpu.PrefetchScalarGridSpec(
        num_scalar_prefetch=1,
        grid=(nseq, n_pages // pages_per_step),
        in_specs=[
            pl.BlockSpec((None, nrow, HEAD_DIM), per_seq),
            pl.BlockSpec((nrow, 1), lambda b, p, pt: (0, 0)),
            pl.BlockSpec((None, n_pages + 1, SUBLANES, PAGE_ROWS), lambda b, p, pt: (b, 0, 0, 0)),
            pl.BlockSpec((None, PAGE_ROWS, HEAD_DIM), per_seq),
            pl.BlockSpec((None, PAGE_ROWS, HEAD_DIM), per_seq),
        ] + [page_spec(i) for i in range(pages_per_step)] + [page_spec(i) for i in range(pages_per_step)],
        out_specs=pl.BlockSpec((None, nrow, HEAD_DIM), per_seq),
        scratch_shapes=[
            pltpu.VMEM((nrow, PAGE_ROWS), F32),
            pltpu.VMEM((nrow, HEAD_DIM), F32),
            pltpu.VMEM((nrow, 1), F32),
            pltpu.VMEM((nrow, 1), F32),
        ],
    )
    return pl.pallas_call(
        kernel,
        grid_spec=grid_spec,
        out_shape=jax.ShapeDtypeStruct((nseq, nrow, HEAD_DIM), F32),
        compiler_params=_compiler_params(("parallel", "arbitrary")),
        name="attn_decode",
    )(page_table, q_rows, slope_col, bias, k_new, v_new,
      *([cache_k] * pages_per_step), *([cache_v] * pages_per_step))


def _pack_in_weights(w_in):
    assert w_in.shape[1] == D_IN
    w_t = w_in.T
    return w_t.astype(BF16), w_t[OFF_QI:OFF_KIW + LANES]


def _alibi_slopes():
    return jnp.exp2(-8.0 * jnp.arange(1, N_HEADS + 1, dtype=F32) / N_HEADS)


def _forward(x_prompt, x_sample, cache_k, cache_v, cache_kidx, page_table, c_prompt, c_sample,
             w_ada, b_ada, norm_ffn1, w1_gate, w1_up, w1_down, norm_mix, w_in, q_norm, k_norm,
             idx_k_norm, w_spatial, b_spatial, v_norm, out_norm_a, out_norm_b, w_out, norm_ffn2,
             w2_gate, w2_up, w2_down):
    batch, seq, d = x_prompt.shape
    nseq, ntok, _ = x_sample.shape
    n_pages = page_table.shape[1]
    n_pool = cache_k.shape[1]
    w_attn = N_HEADS * HEAD_DIM
    w_g = N_GROUPS * GROUP_DIM
    assert ntok == SUBLANES and seq % 256 == 0 and cache_k.shape[0] == 1
    tq = 256
    tm_ffn = next(t for t in (1024, 512, 256) if seq % t == 0)
    m_s = nseq * ntok

    w_in_t, w_idx = _pack_in_weights(w_in[0])
    wo = w_out[0].astype(BF16)

    n_c = batch + nseq
    c_all = jnp.concatenate([c_prompt, c_sample, jnp.zeros((-n_c % 8, d), F32)], axis=0)
    mods = _ada_call(c_all, w_ada[0], b_ada[0])
    mods_p = [mods[:batch, i * d:(i + 1) * d].reshape(batch, 1, d) for i in range(N_MOD)]
    mods_s = [jnp.repeat(mods[batch:n_c, i * d:(i + 1) * d], ntok, axis=0).reshape(1, m_s, d) for i in range(N_MOD)]

    def stream(x, mod, w1, tm_f, tm_p, tiles_f, tiles_p, decode):
        sh1, sc1, g1, sh2, sc2, g2, sh3, sc3, g3 = mod
        h = _ffn_call(x, sh1, sc1, g1, norm_ffn1[0], *w1, tm=tm_f, tiles_per_group=tiles_f, emit_bf16=decode)
        if decode:
            h, *w1 = h
        proj = _inproj_call(h, sh2, sc2, norm_mix[0], w_in_t, q_norm[0], k_norm[0],
                            idx_k_norm[0], v_norm[0], tm=tm_p, tiles_per_group=tiles_p, decode=decode)
        return h, proj, (g2, sh3, sc3, g3), w1

    def finish(h, out_a, u, vb, rest, w2, ws, bs_t, tm_f, tm_p, tiles_f, tiles_p, unit, decode):
        g2, sh3, sc3, g3 = rest
        h2 = _outproj_call(out_a, u, vb, h, g2, ws, bs_t, out_norm_a[0], out_norm_b[0], wo,
                           tm=tm_p, tiles_per_group=tiles_p, unit=unit)
        return _ffn_call(h2, sh3, sc3, g3, norm_ffn2[0], *w2, tm=tm_f, tiles_per_group=tiles_f,
                         emit_bf16=decode)

    xs = x_sample.reshape(m_s, d)
    h_s, proj_s, rest_s, w1_bf16 = stream(xs, mods_s, (w1_gate[0], w1_up[0], w1_down[0]), m_s, m_s, 1, 1, True)
    q_s, k_s, kb_s, v_s, u_s, vb_s = proj_s
    qi_s, ki_s, wi_s = _idxproj_call(h_s, mods_s[3], mods_s[4], norm_mix[0], w_idx, idx_k_norm[0])
    q_rows = q_s.reshape(nseq, ntok, N_HEADS, HEAD_DIM).transpose(0, 2, 1, 3).reshape(
        nseq, N_HEADS * ntok, HEAD_DIM)
    qi2d = qi_s.reshape(nseq, ntok, IDX_HEADS, IDX_DIM).transpose(0, 2, 1, 3).reshape(
        nseq, IDX_HEADS * ntok, IDX_DIM)
    wcol = wi_s.reshape(nseq, ntok, IDX_HEADS).transpose(0, 2, 1).reshape(nseq, IDX_HEADS * ntok, 1)

    def pad_rows(a, rows):
        a = a.reshape(nseq, -1, a.shape[-1])
        return jnp.pad(a, ((0, 0), (0, rows - a.shape[1]), (0, 0)))

    bias = _select_decode_call(page_table, qi2d, wcol, pad_rows(ki_s, PAGE_SIZE).transpose(0, 2, 1),
                               cache_kidx[0].transpose(0, 2, 1), pages_per_step=math.gcd(n_pages, 16))
    slope_col = jnp.repeat(_alibi_slopes(), ntok).reshape(N_HEADS * ntok, 1)
    out_a_s = _attn_decode_call(page_table, q_rows, slope_col, bias,
                                pad_rows(kb_s.reshape(m_s * N_HEADS, HEAD_DIM), PAGE_ROWS),
                                pad_rows(v_s.astype(BF16).reshape(m_s * N_HEADS, HEAD_DIM), PAGE_ROWS),
                                cache_k.reshape(n_pool, PAGE_ROWS, HEAD_DIM),
                                cache_v.reshape(n_pool, PAGE_ROWS, HEAD_DIM),
                                pages_per_step=math.gcd(n_pages, 8))
    out_a_s = out_a_s.reshape(nseq, N_HEADS, ntok, HEAD_DIM).transpose(0, 2, 1, 3).reshape(m_s, w_attn)
    ws_s = jnp.einsum("ab,gts->gatbs", jnp.eye(nseq, dtype=F32), w_spatial[0][:, :ntok, :ntok]).reshape(
        N_GROUPS, m_s, m_s)
    bs_s = jnp.tile(b_spatial[0][:, :ntok].T, (nseq, 1))
    y_s, *w2_bf16 = finish(h_s, out_a_s, u_s, vb_s, rest_s, (w2_gate[0], w2_up[0], w2_down[0]), ws_s, bs_s,
                           m_s, m_s, 1, 1, m_s, True)

    xp = x_prompt.reshape(batch * seq, d)
    tf_p, tp_p = seq // tm_ffn, seq // tq
    h_p, proj_p, rest_p, _ = stream(xp, mods_p, w1_bf16, tm_ffn, tq, tf_p, tp_p, False)
    qT, qiT, wT, vT, k_p, kb_p, v_p, ki_p, kib_p, u_p, vb_p = proj_p
    out_a_p = _attn_prompt_call(qT, qiT, wT, kb_p, vT, kib_p, batch=batch, seq=seq, tq=tq)
    y_p = finish(h_p, out_a_p, u_p, vb_p, rest_p, w2_bf16, w_spatial[0], b_spatial[0].T,
                 tm_ffn, tq, tf_p, tp_p, CHUNK, False)

    return (
        y_p.reshape(batch, seq, d),
        y_s.reshape(nseq, ntok, d),
        k_p.reshape(1, batch, seq, N_HEADS, HEAD_DIM),
        v_p.reshape(1, batch, seq, N_HEADS, HEAD_DIM),
        ki_p.reshape(1, batch, seq, IDX_DIM),
        k_s.reshape(1, nseq, ntok, N_HEADS, HEAD_DIM),
        v_s.reshape(1, nseq, ntok, N_HEADS, HEAD_DIM),
        ki_s.reshape(1, nseq, ntok, IDX_DIM),
        vb_s.reshape(1, nseq, ntok, w_g),
    )


def kernel(x_prompt, x_sample, cache_k, cache_v, cache_kidx, page_table, c_prompt, c_sample, w_ada, b_ada, norm_ffn1, w1_gate, w1_up, w1_down, norm_mix, w_in, q_norm, k_norm, idx_k_norm, w_spatial, b_spatial, v_norm, out_norm_a, out_norm_b, w_out, norm_ffn2, w2_gate, w2_up, w2_down):
    return _forward(x_prompt, x_sample, cache_k, cache_v, cache_kidx, page_table, c_prompt, c_sample,
                    w_ada, b_ada, norm_ffn1, w1_gate, w1_up, w1_down, norm_mix, w_in, q_norm, k_norm,
                    idx_k_norm, w_spatial, b_spatial, v_norm, out_norm_a, out_norm_b, w_out, norm_ffn2,
                    w2_gate, w2_up, w2_down)
```

```python
import functools
import math

import jax
import jax.numpy as jnp
from jax import lax
from jax.experimental import pallas as pl
from jax.experimental.pallas import tpu as pltpu

F32 = jnp.float32
BF16 = jnp.bfloat16
I32 = jnp.int32

EPS = 1e-6
HEAD_DIM = 128
N_HEADS = 8
N_GROUPS = 8
GROUP_DIM = 128
CHUNK = 128
IDX_HEADS = 16
IDX_DIM = 64
TOPK_MAX = 256
PAGE_SIZE = 128
N_MOD = 9
SUBLANES = 8
PAGE_ROWS = PAGE_SIZE * N_HEADS

INT_MIN = -(2 ** 31)
INT_MAX = 2 ** 31 - 1
F32_MIN = float(jnp.finfo(jnp.float32).min)
LOG2E = math.log2(math.e)
QK_SCALE = (HEAD_DIM ** -0.5) * LOG2E

VMEM_LIMIT_BYTES = 60000 * 1024
FFN_VMEM_LIMIT_BYTES = 62 * 1024 * 1024

NT_DIMS = (((1,), (1,)), ((), ()))


def _compiler_params(semantics, vmem_limit_bytes=VMEM_LIMIT_BYTES):
    return pltpu.CompilerParams(dimension_semantics=semantics, vmem_limit_bytes=vmem_limit_bytes)


def _modulate(x, gain, shift, scale):
    ms = jnp.mean(x * x, axis=-1, keepdims=True)
    y = x * lax.rsqrt(ms + EPS)
    return (y * gain) * (1.0 + scale) + shift


def _rms_rows(x, gain):
    ms = jnp.mean(x * x, axis=-1, keepdims=True)
    return (x * lax.rsqrt(ms + EPS)) * gain


def _div_pow2(x, n):
    assert n & (n - 1) == 0
    return lax.shift_right_logical(x, jnp.int32(n.bit_length() - 1))


def _mod_pow2(x, n):
    assert n & (n - 1) == 0
    return x & jnp.int32(n - 1)


LANES = 128


def _fold_lane_tiles(x, op):
    return functools.reduce(op, [x[:, i * LANES:(i + 1) * LANES] for i in range(x.shape[1] // LANES)])


def _rms_cols(x, gain_col):
    ms = jnp.mean(x * x, axis=0, keepdims=True)
    return (x * lax.rsqrt(ms + EPS)) * gain_col


def _ada_kernel(c_ref, w_ref, b_ref, o_ref):
    a = jax.nn.silu(c_ref[...]).astype(BF16)
    o_ref[...] = jnp.dot(a, w_ref[...].astype(BF16), preferred_element_type=F32) + b_ref[...]


def _ada_call(c_all, w_ada, b_ada):
    rows, d = c_all.shape
    n = w_ada.shape[1]
    tn = 1024
    return pl.pallas_call(
        _ada_kernel,
        grid=(n // tn,),
        in_specs=[
            pl.BlockSpec((rows, d), lambda j: (0, 0)),
            pl.BlockSpec((d, tn), lambda j: (0, j)),
            pl.BlockSpec((1, tn), lambda j: (0, j)),
        ],
        out_specs=pl.BlockSpec((rows, tn), lambda j: (0, j)),
        out_shape=jax.ShapeDtypeStruct((rows, n), F32),
        compiler_params=_compiler_params(("parallel",)),
        name="ada_proj",
    )(c_all, w_ada, b_ada.reshape(1, n))


def _ffn_kernel(x_ref, sh_ref, sc_ref, g_ref, nw_ref, wg_ref, wu_ref, wd_ref, o_ref, *rest, emit_bf16):
    if emit_bf16:
        wgb_ref, wub_ref, wdb_ref, xn_ref = rest
    else:
        (xn_ref,) = rest
    n = pl.program_id(1)

    @pl.when(n == 0)
    def _():
        xn_ref[...] = _modulate(x_ref[...], nw_ref[...], sh_ref[...], sc_ref[...]).astype(BF16)
        o_ref[...] = jnp.zeros_like(o_ref)

    wg, wu, wd = wg_ref[...].astype(BF16), wu_ref[...].astype(BF16), wd_ref[...].astype(BF16)
    if emit_bf16:
        wgb_ref[...] = wg
        wub_ref[...] = wu
        wdb_ref[...] = wd
    xn = xn_ref[...]
    gate = jnp.dot(xn, wg, preferred_element_type=F32)
    up = jnp.dot(xn, wu, preferred_element_type=F32)
    hmid = (jax.nn.silu(gate) * up).astype(BF16)
    o_ref[...] += jnp.dot(hmid, wd, preferred_element_type=F32)

    @pl.when(n == pl.num_programs(1) - 1)
    def _():
        o_ref[...] = x_ref[...] + (0.5 * g_ref[...]) * o_ref[...]


def _mod_spec(mod, tiles_per_group):
    _, r, d = mod.shape
    return pl.BlockSpec((None, r, d), lambda i, *_: (i // tiles_per_group, 0, 0))


def _ffn_call(x, shift, scale, gate, norm_w, wg, wu, wd, *, tm, tiles_per_group, emit_bf16=False):
    m, d = x.shape
    dff = wg.shape[1]
    tn = 512
    assert not emit_bf16 or m == tm
    col = lambda: pl.BlockSpec((d, tn), lambda i, n: (0, n))
    row = lambda: pl.BlockSpec((tn, d), lambda i, n: (n, 0))
    out_specs = [pl.BlockSpec((tm, d), lambda i, n: (i, 0))]
    out_shape = [jax.ShapeDtypeStruct((m, d), F32)]
    if emit_bf16:
        out_specs += [col(), col(), row()]
        out_shape += [jax.ShapeDtypeStruct(w.shape, BF16) for w in (wg, wu, wd)]
    outs = pl.pallas_call(
        functools.partial(_ffn_kernel, emit_bf16=emit_bf16),
        grid=(m // tm, dff // tn),
        in_specs=[
            pl.BlockSpec((tm, d), lambda i, n: (i, 0)),
            _mod_spec(shift, tiles_per_group),
            _mod_spec(scale, tiles_per_group),
            _mod_spec(gate, tiles_per_group),
            pl.BlockSpec((1, d), lambda i, n: (0, 0)),
            col(), col(), row(),
        ],
        out_specs=out_specs,
        out_shape=out_shape,
        scratch_shapes=[pltpu.VMEM((tm, d), BF16)],
        compiler_params=_compiler_params(("parallel", "arbitrary"), FFN_VMEM_LIMIT_BYTES),
        name="ffn",
    )(x, shift, scale, gate, norm_w.reshape(1, d), wg, wu, wd)
    return outs if emit_bf16 else outs[0]


W_ATTN = N_HEADS * HEAD_DIM
W_IDX = IDX_HEADS * IDX_DIM
W_GATE = N_GROUPS * GROUP_DIM
OFF_Q, OFF_K, OFF_V = 0, W_ATTN, 2 * W_ATTN
OFF_QI = 3 * W_ATTN
OFF_KIW = OFF_QI + W_IDX
OFF_U = OFF_KIW + IDX_DIM + IDX_HEADS
OFF_VB = OFF_U + W_GATE
D_IN = OFF_VB + W_GATE
assert OFF_U % 16 == 0 and OFF_VB % 16 == 0


def _head_rows(h, tm):
    return pl.ds(h, tm, stride=N_HEADS)


def _inproj_kernel(h_ref, sh_ref, sc_ref, nw_ref, w_ref, qn_ref, kn_ref, ikn_ref, vn_ref,
                   qT_ref, qiT_ref, wT_ref, vT_ref, k_ref, kb_ref, v_ref, ki_ref, kib_ref, u_ref, vb_ref):
    tm = h_ref.shape[0]
    a = _modulate(h_ref[...], nw_ref[...], sh_ref[...], sc_ref[...]).astype(BF16)

    def proj(off, rows):
        return lax.dot_general(w_ref[off:off + rows, :], a, NT_DIMS, preferred_element_type=F32)

    q_t = proj(OFF_Q, W_ATTN)
    k_t = proj(OFF_K, W_ATTN)
    v_t = proj(OFF_V, W_ATTN)
    for h in range(N_HEADS):
        rows = slice(h * HEAD_DIM, (h + 1) * HEAD_DIM)
        qT_ref[rows, :] = (_rms_cols(q_t[rows, :], qn_ref[...]) * QK_SCALE).astype(BF16)
        kh = _rms_cols(k_t[rows, :], kn_ref[...]).T
        kb_ref[:, rows] = kh.astype(BF16)
        k_ref[_head_rows(h, tm), :] = kh
        v_ref[_head_rows(h, tm), :] = v_t[rows, :].T
    vT_ref[...] = v_t.astype(BF16)
    qiT_ref[...] = (proj(OFF_QI, W_IDX) * (IDX_DIM ** -0.5)).astype(BF16)
    tail = proj(OFF_KIW, LANES)
    wT_ref[...] = tail[IDX_DIM:IDX_DIM + IDX_HEADS, :] * (IDX_HEADS ** -0.5)
    ki_t = _rms_cols(tail[0:IDX_DIM, :], ikn_ref[...])
    ki = jnp.concatenate([ki_t, tail[IDX_DIM:, :]], axis=0).T[:, 0:IDX_DIM]
    ki_ref[...] = ki
    kib_ref[...] = ki.astype(BF16)
    u_t = proj(OFF_U, W_GATE)
    vb_t = proj(OFF_VB, W_GATE)
    for g in range(N_GROUPS):
        rows = slice(g * GROUP_DIM, (g + 1) * GROUP_DIM)
        u_ref[:, rows] = jax.nn.gelu(u_t[rows, :]).T
        vb_ref[:, rows] = _rms_cols(jax.nn.gelu(vb_t[rows, :]), vn_ref[:, g:g + 1]).T


def _inproj_decode_kernel(h_ref, sh_ref, sc_ref, nw_ref, w_ref, qn_ref, kn_ref, vn_ref,
                          q_ref, k_ref, kb_ref, v_ref, u_ref, vb_ref):
    tm = h_ref.shape[0]
    a = _modulate(h_ref[...], nw_ref[...], sh_ref[...], sc_ref[...]).astype(BF16)

    def proj(off, width):
        return lax.dot_general(a, w_ref[off:off + width, :], NT_DIMS, preferred_element_type=F32)

    q = proj(OFF_Q, W_ATTN)
    k = proj(OFF_K, W_ATTN)
    v = proj(OFF_V, W_ATTN)
    for h in range(N_HEADS):
        cols = slice(h * HEAD_DIM, (h + 1) * HEAD_DIM)
        q_ref[:, cols] = (_rms_rows(q[:, cols], qn_ref[...]) * QK_SCALE).astype(BF16)
        kh = _rms_rows(k[:, cols], kn_ref[...])
        kb_ref[:, cols] = kh.astype(BF16)
        k_ref[_head_rows(h, tm), :] = kh
        v_ref[_head_rows(h, tm), :] = v[:, cols]
    u_ref[...] = jax.nn.gelu(proj(OFF_U, W_GATE))
    vb = jax.nn.gelu(proj(OFF_VB, W_GATE))
    for g in range(N_GROUPS):
        cols = slice(g * GROUP_DIM, (g + 1) * GROUP_DIM)
        vb_ref[:, cols] = _rms_rows(vb[:, cols], vn_ref[g:g + 1, :])


def _inproj_call(h, shift, scale, norm_w, w_t, q_norm, k_norm, idx_k_norm, v_norm, *, tm, tiles_per_group,
                 decode):
    m, d = h.shape
    nt = m // tm
    row = lambda i: (i, 0)
    fm = lambda i: (i, 0, 0)
    token_major = lambda width, dtype: (jax.ShapeDtypeStruct((m, width), dtype), pl.BlockSpec((tm, width), row))
    feature_major = lambda width, dtype: (jax.ShapeDtypeStruct((nt, width, tm), dtype),
                                          pl.BlockSpec((None, width, tm), fm))
    head_major = lambda: (jax.ShapeDtypeStruct((m * N_HEADS, HEAD_DIM), F32),
                          pl.BlockSpec((tm * N_HEADS, HEAD_DIM), row))
    if decode:
        outs = [token_major(W_ATTN, BF16)]
        gains = [q_norm.reshape(1, HEAD_DIM), k_norm.reshape(1, HEAD_DIM), v_norm]
    else:
        outs = [feature_major(W_ATTN, BF16), feature_major(W_IDX, BF16),
                feature_major(IDX_HEADS, F32), feature_major(W_ATTN, BF16)]
        gains = [q_norm.reshape(HEAD_DIM, 1), k_norm.reshape(HEAD_DIM, 1), idx_k_norm.reshape(IDX_DIM, 1),
                 v_norm.T]
    outs += [head_major(), token_major(W_ATTN, BF16), head_major()]
    if not decode:
        outs += [token_major(IDX_DIM, F32), token_major(IDX_DIM, BF16)]
    outs += [token_major(W_GATE, F32), token_major(W_GATE, F32)]
    const2d = lambda a: pl.BlockSpec(a.shape, lambda i: (0, 0))
    return pl.pallas_call(
        _inproj_decode_kernel if decode else _inproj_kernel,
        grid=(nt,),
        in_specs=[
            pl.BlockSpec((tm, d), row),
            _mod_spec(shift, tiles_per_group),
            _mod_spec(scale, tiles_per_group),
            pl.BlockSpec((1, d), lambda i: (0, 0)),
            const2d(w_t),
        ] + [const2d(g) for g in gains],
        out_specs=[spec for _, spec in outs],
        out_shape=[shape for shape, _ in outs],
        compiler_params=_compiler_params(("parallel",)),
        name="in_proj",
    )(h, shift, scale, norm_w.reshape(1, d), w_t, *gains)


def _dot3(a, b, dims=(((1,), (0,)), ((), ()))):
    a_hi = a.astype(BF16)
    b_hi = b.astype(BF16)
    a_lo = (a - a_hi.astype(F32)).astype(BF16)
    b_lo = (b - b_hi.astype(F32)).astype(BF16)
    dot = functools.partial(lax.dot_general, dimension_numbers=dims, preferred_element_type=F32)
    return dot(a_hi, b_hi) + (dot(a_lo, b_hi) + dot(a_hi, b_lo))


def _idxproj_kernel(h_ref, sh_ref, sc_ref, nw_ref, w_ref, ikn_ref, qi_ref, ki_ref, wi_ref):
    a = _modulate(h_ref[...], nw_ref[...], sh_ref[...], sc_ref[...])
    r = _dot3(a, w_ref[...], NT_DIMS)
    qi_ref[...] = r[:, 0:W_IDX] * (IDX_DIM ** -0.5)
    kiw = r[:, W_IDX:W_IDX + LANES]
    ki_ref[...] = _rms_rows(kiw[:, 0:IDX_DIM], ikn_ref[...])
    wi_ref[...] = kiw[:, IDX_DIM:IDX_DIM + IDX_HEADS] * (IDX_HEADS ** -0.5)


def _idxproj_call(h, shift, scale, norm_w, w_idx, idx_k_norm):
    m, d = h.shape
    full = lambda a: pl.BlockSpec(a.shape, lambda i: (0,) * a.ndim)
    args = (h, shift[0], scale[0], norm_w.reshape(1, d), w_idx, idx_k_norm.reshape(1, IDX_DIM))
    return pl.pallas_call(
        _idxproj_kernel,
        grid=(1,),
        in_specs=[full(a) for a in args],
        out_specs=[pl.BlockSpec((m, W_IDX), lambda i: (0, 0)), pl.BlockSpec((m, IDX_DIM), lambda i: (0, 0)),
                   pl.BlockSpec((m, IDX_HEADS), lambda i: (0, 0))],
        out_shape=[jax.ShapeDtypeStruct((m, W_IDX), F32), jax.ShapeDtypeStruct((m, IDX_DIM), F32),
                   jax.ShapeDtypeStruct((m, IDX_HEADS), F32)],
        compiler_params=_compiler_params(("arbitrary",)),
        name="idx_proj_decode",
    )(*args)


def _key_to_float(key):
    return lax.bitcast_convert_type(jnp.where(key < 0, key ^ jnp.int32(INT_MAX), key), F32)


def _topk_threshold(count_ge, topk, like):
    zero = jnp.zeros_like(like)
    cnt0 = count_ge(_key_to_float(zero))
    nonneg = cnt0 >= topk
    t0 = jnp.where(nonneg, zero, jnp.int32(INT_MIN))
    c0 = jnp.where(nonneg, cnt0, jnp.int32(topk))

    def body(p, carry):
        t, ct = carry
        cand = t | jnp.left_shift(jnp.int32(1), 31 - p)
        cnt = count_ge(_key_to_float(cand))
        take = cnt >= topk
        return jnp.where(take, cand, t), jnp.where(take, cnt, ct)

    t, ct = lax.fori_loop(1, 32, body, (t0, c0))
    return t != jnp.int32(INT_MIN), _key_to_float(t), ct


def _tie_cutoff(count_where, thr, topk, nbits, excess):
    need = topk - count_where(lambda sc, pos: sc > thr)
    j = jnp.zeros_like(need)
    for bit in reversed(range(nbits)):
        cand = j | jnp.int32(1 << bit)
        cnt = count_where(lambda sc, pos, cand=cand: (sc == thr) & (pos < cand))
        j = jnp.where(cnt < need, cand, j)
    return jnp.where(excess, j, jnp.int32(INT_MAX))


def _select(count_where, topk, pos_bits, like):
    has_thr, thr, cnt_thr = _topk_threshold(lambda t: count_where(lambda sc, pos: sc >= t), topk, like)
    excess = (cnt_thr > topk) & has_thr
    cutoff = lax.cond(
        jnp.max(excess.astype(I32)) > 0,
        lambda: _tie_cutoff(count_where, thr, topk, pos_bits, excess),
        lambda: jnp.full(like.shape, INT_MAX, I32),
    )
    return lambda sc, pos: jnp.logical_not(has_thr) | (sc > thr) | ((sc == thr) & (pos <= cutoff))


def _attn_prompt_kernel(qT_ref, qiT_ref, wT_ref, kb_ref, vT_ref, kib_ref, o_ref,
                        sc_ref, acc_ref, m_ref, l_ref, a_ref, s_ref, p_ref, *, tq, topk, pos_bits):
    j = pl.program_id(1)
    nchunks = j + 1
    groups = tq // SUBLANES
    t_pos = j * tq + lax.broadcasted_iota(I32, (1, tq), 1)
    s_off = lax.broadcasted_iota(I32, (tq, 1), 0)

    def chunk_start(c):
        return pl.multiple_of(c * tq, tq)

    def score_body(c, carry):
        off = chunk_start(c)
        kic = kib_ref[pl.ds(off, tq), :]
        sc = jnp.zeros((tq, tq), F32)
        for h in range(IDX_HEADS):
            d = jnp.dot(kic, qiT_ref[h * IDX_DIM:(h + 1) * IDX_DIM, :], preferred_element_type=F32)
            sc = sc + jnp.maximum(d, 0.0) * wT_ref[h:h + 1, :]
        sc_ref[pl.ds(off, tq), :] = jnp.where((off + s_off) <= t_pos, sc, -jnp.inf)
        return carry

    lax.fori_loop(0, nchunks, score_body, 0)

    def count_where(pred):
        def body(c, cnt):
            off = chunk_start(c)
            hit = pred(sc_ref[pl.ds(off, tq), :], off + s_off).reshape(groups, SUBLANES, tq)
            accs = [cnt] + [jnp.zeros_like(cnt)] * 3
            for g in range(groups):
                accs[g % 4] = jnp.where(hit[g], accs[g % 4] + 1, accs[g % 4])
            return (accs[0] + accs[1]) + (accs[2] + accs[3])

        cnt8 = lax.fori_loop(0, nchunks, body, jnp.zeros((SUBLANES, tq), I32))
        return jnp.sum(cnt8, axis=0, keepdims=True)

    selected = _select(count_where, topk, pos_bits, t_pos)

    m_ref[...] = jnp.full(m_ref.shape, F32_MIN, F32)
    l_ref[...] = jnp.zeros_like(l_ref)
    acc_ref[...] = jnp.zeros_like(acc_ref)

    def att_body(c, carry):
        off = chunk_start(c)
        pos = off + s_off
        sel = selected(sc_ref[pl.ds(off, tq), :], pos) & (pos <= t_pos)
        dmc = jnp.where(sel, (t_pos - pos).astype(F32) * LOG2E, jnp.inf)
        for h in range(N_HEADS):
            rows = slice(h * HEAD_DIM, (h + 1) * HEAD_DIM)
            qk = jnp.dot(kb_ref[pl.ds(off, tq), rows], qT_ref[rows, :], preferred_element_type=F32)
            s = qk - (2.0 ** -(h + 1)) * dmc
            s_ref[h] = s
            cm = jnp.max(s.reshape(groups, SUBLANES, tq), axis=0)
            m_old = m_ref[h]
            m_new = jnp.maximum(m_old, jnp.broadcast_to(jnp.max(cm, axis=0, keepdims=True), cm.shape))
            a_ref[h] = jnp.exp2(m_old - m_new)
            m_ref[h] = m_new
        for h in range(N_HEADS):
            p = jnp.exp2(s_ref[h].reshape(groups, SUBLANES, tq) - m_ref[h][None])
            l_ref[h] = a_ref[h] * l_ref[h] + jnp.sum(p, axis=0)
            p_ref[h] = p.reshape(tq, tq).astype(BF16)
        for h in range(N_HEADS):
            rows = slice(h * HEAD_DIM, (h + 1) * HEAD_DIM)
            pv = jnp.dot(vT_ref[c, rows, :], p_ref[h], preferred_element_type=F32)
            acc = acc_ref[rows, :].reshape(HEAD_DIM // SUBLANES, SUBLANES, tq) * a_ref[h][None]
            acc_ref[rows, :] = acc.reshape(HEAD_DIM, tq) + pv
        return carry

    lax.fori_loop(0, nchunks, att_body, 0)

    for h in range(N_HEADS):
        rows = slice(h * HEAD_DIM, (h + 1) * HEAD_DIM)
        acc_ref[rows, :] = acc_ref[rows, :] / jnp.sum(l_ref[h], axis=0, keepdims=True)
    o_ref[...] = acc_ref[...].T


def _attn_prompt_call(qT, qiT, wT, kb, vT, kib, *, batch, seq, tq):
    w_attn = N_HEADS * HEAD_DIM
    w_idx = IDX_HEADS * IDX_DIM
    nq = seq // tq
    topk = min(TOPK_MAX, seq // 4)
    qblk = lambda b, j: (b * nq + j, 0, 0)
    kernel = functools.partial(_attn_prompt_kernel, tq=tq, topk=topk, pos_bits=max(1, (seq - 1).bit_length()))
    return pl.pallas_call(
        kernel,
        grid=(batch, nq),
        in_specs=[
            pl.BlockSpec((None, w_attn, tq), qblk),
            pl.BlockSpec((None, w_idx, tq), qblk),
            pl.BlockSpec((None, IDX_HEADS, tq), qblk),
            pl.BlockSpec((seq, w_attn), lambda b, j: (b, 0), pipeline_mode=pl.Buffered(1)),
            pl.BlockSpec((nq, w_attn, tq), lambda b, j: (b, 0, 0), pipeline_mode=pl.Buffered(1)),
            pl.BlockSpec((seq, IDX_DIM), lambda b, j: (b, 0), pipeline_mode=pl.Buffered(1)),
        ],
        out_specs=pl.BlockSpec((tq, w_attn), lambda b, j: (b * nq + j, 0)),
        out_shape=jax.ShapeDtypeStruct((batch * seq, w_attn), F32),
        scratch_shapes=[
            pltpu.VMEM((seq, tq), F32),
            pltpu.VMEM((w_attn, tq), F32),
            pltpu.VMEM((N_HEADS, SUBLANES, tq), F32),
            pltpu.VMEM((N_HEADS, SUBLANES, tq), F32),
            pltpu.VMEM((N_HEADS, SUBLANES, tq), F32),
            pltpu.VMEM((N_HEADS, tq, tq), F32),
            pltpu.VMEM((N_HEADS, tq, tq), BF16),
        ],
        compiler_params=_compiler_params(("parallel", "arbitrary")),
        name="attn_prompt",
    )(qT, qiT, wT, kb, vT, kib)


def _outproj_kernel(oa_ref, u_ref, vb_ref, h_ref, g_ref, ws_ref, bs_ref, na_ref, nb_ref, wo_ref, o_ref, *, unit):
    tm = oa_ref.shape[0]
    w_attn = N_HEADS * HEAD_DIM
    rows_i = lax.broadcasted_iota(I32, (unit, unit), 0)
    cols_i = lax.broadcasted_iota(I32, (unit, unit), 1)
    causal = rows_i >= cols_i
    ob_parts = []
    for g in range(N_GROUPS):
        cols = slice(g * GROUP_DIM, (g + 1) * GROUP_DIM)
        wm = jnp.where(causal, ws_ref[g], 0.0).astype(BF16)
        bias = bs_ref[:, g:g + 1]
        mixed = []
        for r in range(tm // unit):
            rws = slice(r * unit, (r + 1) * unit)
            mixed.append(jnp.dot(wm, vb_ref[rws, cols].astype(BF16), preferred_element_type=F32) + bias)
        mixed = mixed[0] if len(mixed) == 1 else jnp.concatenate(mixed, axis=0)
        ob_parts.append(u_ref[:, cols] * mixed)
    ob = jnp.concatenate(ob_parts, axis=1)
    za = _rms_rows(oa_ref[...], na_ref[...]).astype(BF16)
    zb = _rms_rows(ob, nb_ref[...]).astype(BF16)
    mix = (jnp.dot(za, wo_ref[0:w_attn, :], preferred_element_type=F32)
           + jnp.dot(zb, wo_ref[w_attn:, :], preferred_element_type=F32))
    o_ref[...] = h_ref[...] + g_ref[...] * mix


def _outproj_call(out_a, u, vb, h, gate, ws, bs_t, norm_a, norm_b, w_out, *, tm, tiles_per_group, unit):
    m, d = h.shape
    w_attn = out_a.shape[1]
    w_g = u.shape[1]
    row = lambda i: (i, 0)
    const = lambda i: (0, 0)
    return pl.pallas_call(
        functools.partial(_outproj_kernel, unit=unit),
        grid=(m // tm,),
        in_specs=[
            pl.BlockSpec((tm, w_attn), row),
            pl.BlockSpec((tm, w_g), row),
            pl.BlockSpec((tm, w_g), row),
            pl.BlockSpec((tm, d), row),
            _mod_spec(gate, tiles_per_group),
            pl.BlockSpec(ws.shape, lambda i: (0, 0, 0)),
            pl.BlockSpec(bs_t.shape, const),
            pl.BlockSpec((1, w_attn), const),
            pl.BlockSpec((1, w_g), const),
            pl.BlockSpec(w_out.shape, const),
        ],
        out_specs=pl.BlockSpec((tm, d), row),
        out_shape=jax.ShapeDtypeStruct((m, d), F32),
        compiler_params=_compiler_params(("parallel",)),
        name="out_proj",
    )(out_a, u, vb, h, gate, ws, bs_t, norm_a.reshape(1, w_attn), norm_b.reshape(1, w_g), w_out)


def _select_decode_kernel(pt_ref, qi_ref, w_ref, kin_ref, *rest, pages_per_step, n_pages, topk, pos_bits):
    page_refs = rest[:pages_per_step]
    bias_ref = rest[pages_per_step]
    sc_ref = rest[pages_per_step + 1]
    p = pl.program_id(1)
    qi = qi_ref[...]
    w = w_ref[...]

    def page_scores(keys_t):
        d = _dot3(qi, keys_t)
        r = jnp.maximum(d, 0.0) * w
        return jnp.sum(r.reshape(IDX_HEADS, SUBLANES, PAGE_SIZE), axis=0)

    for i in range(pages_per_step):
        sc_ref[p * pages_per_step + i] = page_scores(page_refs[i][...])

    @pl.when(p == pl.num_programs(1) - 1)
    def _():
        shape = (n_pages + 1, SUBLANES, PAGE_SIZE)
        tok = lax.broadcasted_iota(I32, (SUBLANES, PAGE_SIZE), 0)
        lane = lax.broadcasted_iota(I32, (SUBLANES, PAGE_SIZE), 1)
        sc_ref[n_pages] = jnp.where(lane <= tok, page_scores(kin_ref[...]), -jnp.inf)
        pos = lax.broadcasted_iota(I32, shape, 0) * PAGE_SIZE + lax.broadcasted_iota(I32, shape, 2)
        q_pos = n_pages * PAGE_SIZE + lax.broadcasted_iota(I32, shape, 1)

        def count_where(pred):
            hit = pred(sc_ref[...], pos).astype(I32)
            return jnp.sum(jnp.sum(hit, axis=0), axis=-1, keepdims=True)

        selected = _select(count_where, topk, pos_bits, jnp.zeros((SUBLANES, 1), I32))
        sel = selected(sc_ref[...], pos) & (pos <= q_pos)
        sel2d = jnp.where(sel, 1.0, 0.0).astype(BF16).reshape((n_pages + 1) * SUBLANES, PAGE_SIZE)
        expand = (_div_pow2(lax.broadcasted_iota(I32, (PAGE_SIZE, PAGE_ROWS), 1), N_HEADS)
                  == lax.broadcasted_iota(I32, (PAGE_SIZE, PAGE_ROWS), 0))
        rep = jnp.dot(sel2d, jnp.where(expand, 1.0, 0.0).astype(BF16), preferred_element_type=F32)
        bias_ref[...] = jnp.where(rep > 0.5, 0.0, -jnp.inf).reshape(n_pages + 1, SUBLANES, PAGE_ROWS)


def _select_decode_call(page_table, qi2d, wcol, ki_new, cache_kidx, *, pages_per_step):
    nseq, n_pages = page_table.shape
    topk = min(TOPK_MAX, (n_pages * PAGE_SIZE + SUBLANES) // 4)
    pos_bits = max(1, ((n_pages + 1) * PAGE_SIZE - 1).bit_length())
    per_seq = lambda b, p, pt: (b, 0, 0)

    def page_spec(i):
        return pl.BlockSpec((None, IDX_DIM, PAGE_SIZE),
                            lambda b, p, pt, i=i: (pt[b, p * pages_per_step + i], 0, 0))

    kernel = functools.partial(_select_decode_kernel, pages_per_step=pages_per_step, n_pages=n_pages,
                               topk=topk, pos_bits=pos_bits)
    grid_spec = pltpu.PrefetchScalarGridSpec(
        num_scalar_prefetch=1,
        grid=(nseq, n_pages // pages_per_step),
        in_specs=[
            pl.BlockSpec((None, IDX_HEADS * SUBLANES, IDX_DIM), per_seq),
            pl.BlockSpec((None, IDX_HEADS * SUBLANES, 1), per_seq),
            pl.BlockSpec((None, IDX_DIM, PAGE_SIZE), per_seq),
        ] + [page_spec(i) for i in range(pages_per_step)],
        out_specs=pl.BlockSpec((None, n_pages + 1, SUBLANES, PAGE_ROWS), lambda b, p, pt: (b, 0, 0, 0)),
        scratch_shapes=[pltpu.VMEM((n_pages + 1, SUBLANES, PAGE_SIZE), F32)],
    )
    return pl.pallas_call(
        kernel,
        grid_spec=grid_spec,
        out_shape=jax.ShapeDtypeStruct((nseq, n_pages + 1, SUBLANES, PAGE_ROWS), F32),
        compiler_params=_compiler_params(("parallel", "arbitrary")),
        name="select_decode",
    )(page_table, qi2d, wcol, ki_new, *([cache_kidx] * pages_per_step))


def _attn_decode_kernel(pt_ref, q_ref, slope_ref, bias_ref, kn_ref, vn_ref, *rest, pages_per_step, n_pages):
    k_refs = rest[:pages_per_step]
    v_refs = rest[pages_per_step:2 * pages_per_step]
    o_ref, a_ref, acc_ref, m_ref, l_ref = rest[2 * pages_per_step:]
    p = pl.program_id(1)
    nrow = N_HEADS * SUBLANES
    slope = slope_ref[...] * LOG2E

    @pl.when(p == 0)
    def _():
        m_ref[...] = jnp.full(m_ref.shape, F32_MIN, F32)
        l_ref[...] = jnp.zeros_like(l_ref)
        acc_ref[...] = jnp.zeros_like(acc_ref)
        row = lax.broadcasted_iota(I32, (nrow, PAGE_ROWS), 0)
        col = lax.broadcasted_iota(I32, (nrow, PAGE_ROWS), 1)
        same_head = _mod_pow2(col, N_HEADS) == _div_pow2(row, SUBLANES)
        tok_minus_key = (_mod_pow2(row, SUBLANES) - _div_pow2(col, N_HEADS)).astype(F32)
        a_ref[...] = jnp.where(same_head, -slope * tok_minus_key, -jnp.inf)

    def logits(keys_bf16, page):
        page_dist = jnp.full((nrow, 1), (n_pages - page) * PAGE_SIZE, I32).astype(F32)
        s = lax.dot_general(q_ref[...], keys_bf16, NT_DIMS, preferred_element_type=F32)
        return s + a_ref[...] - slope * page_dist + jnp.concatenate([bias_ref[page]] * N_HEADS, axis=0)

    def update(s_pages, v_pages):
        lanes_max = functools.reduce(jnp.maximum, [_fold_lane_tiles(s, jnp.maximum) for s in s_pages])
        m_old = m_ref[...]
        m_new = jnp.maximum(m_old, jnp.max(lanes_max, axis=-1, keepdims=True))
        alpha = jnp.exp2(m_old - m_new)
        lanes_sum = jnp.zeros((nrow, PAGE_SIZE), F32)
        acc = alpha * acc_ref[...]
        for s, vals in zip(s_pages, v_pages):
            pr = jnp.exp2(s - m_new)
            lanes_sum = lanes_sum + _fold_lane_tiles(pr, jnp.add)
            acc = acc + jnp.dot(pr.astype(BF16), vals, preferred_element_type=F32)
        l_ref[...] = alpha * l_ref[...] + jnp.sum(lanes_sum, axis=-1, keepdims=True)
        acc_ref[...] = acc
        m_ref[...] = m_new

    update([logits(k_refs[i][...].astype(BF16), p * pages_per_step + i) for i in range(pages_per_step)],
           [v_refs[i][...].astype(BF16) for i in range(pages_per_step)])

    @pl.when(p == pl.num_programs(1) - 1)
    def _():
        update([logits(kn_ref[...], n_pages)], [vn_ref[...]])
        o_ref[...] = acc_ref[...] / l_ref[...]


def _attn_decode_call(page_table, q_rows, slope_col, bias, k_new, v_new, cache_k, cache_v, *, pages_per_step):
    nseq, n_pages = page_table.shape
    nrow = N_HEADS * SUBLANES
    per_seq = lambda b, p, pt: (b, 0, 0)

    def page_spec(i):
        return pl.BlockSpec((None, PAGE_ROWS, HEAD_DIM),
                            lambda b, p, pt, i=i: (pt[b, p * pages_per_step + i], 0, 0))

    kernel = functools.partial(_attn_decode_kernel, pages_per_step=pages_per_step, n_pages=n_pages)
    grid_spec = pltpu.PrefetchScalarGridSpec(
        num_scalar_prefetch=1,
        grid=(nseq, n_pages // pages_per_step),
        in_specs=[
            pl.BlockSpec((None, nrow, HEAD_DIM), per_seq),
            pl.BlockSpec((nrow, 1), lambda b, p, pt: (0, 0)),
            pl.BlockSpec((None, n_pages + 1, SUBLANES, PAGE_ROWS), lambda b, p, pt: (b, 0, 0, 0)),
            pl.BlockSpec((None, PAGE_ROWS, HEAD_DIM), per_seq),
            pl.BlockSpec((None, PAGE_ROWS, HEAD_DIM), per_seq),
        ] + [page_spec(i) for i in range(pages_per_step)] + [page_spec(i) for i in range(pages_per_step)],
        out_specs=pl.BlockSpec((None, nrow, HEAD_DIM), per_seq),
        scratch_shapes=[
            pltpu.VMEM((nrow, PAGE_ROWS), F32),
            pltpu.VMEM((nrow, HEAD_DIM), F32),
            pltpu.VMEM((nrow, 1), F32),
            pltpu.VMEM((nrow, 1), F32),
        ],
    )
    return pl.pallas_call(
        kernel,
        grid_spec=grid_spec,
        out_shape=jax.ShapeDtypeStruct((nseq, nrow, HEAD_DIM), F32),
        compiler_params=_compiler_params(("parallel", "arbitrary")),
        name="attn_decode",
    )(page_table, q_rows, slope_col, bias, k_new, v_new,
      *([cache_k] * pages_per_step), *([cache_v] * pages_per_step))


def _pack_in_weights(w_in):
    assert w_in.shape[1] == D_IN
    w_t = w_in.T
    return w_t.astype(BF16), w_t[OFF_QI:OFF_KIW + LANES]


def _alibi_slopes():
    return jnp.exp2(-8.0 * jnp.arange(1, N_HEADS + 1, dtype=F32) / N_HEADS)


def _forward(x_prompt, x_sample, cache_k, cache_v, cache_kidx, page_table, c_prompt, c_sample,
             w_ada, b_ada, norm_ffn1, w1_gate, w1_up, w1_down, norm_mix, w_in, q_norm, k_norm,
             idx_k_norm, w_spatial, b_spatial, v_norm, out_norm_a, out_norm_b, w_out, norm_ffn2,
             w2_gate, w2_up, w2_down):
    batch, seq, d = x_prompt.shape
    nseq, ntok, _ = x_sample.shape
    n_pages = page_table.shape[1]
    n_pool = cache_k.shape[1]
    w_attn = N_HEADS * HEAD_DIM
    w_g = N_GROUPS * GROUP_DIM
    assert ntok == SUBLANES and seq % 256 == 0 and cache_k.shape[0] == 1
    tq = 256
    tm_ffn = next(t for t in (1024, 512, 256) if seq % t == 0)
    m_s = nseq * ntok

    w_in_t, w_idx = _pack_in_weights(w_in[0])
    wo = w_out[0].astype(BF16)

    n_c = batch + nseq
    c_all = jnp.concatenate([c_prompt, c_sample, jnp.zeros((-n_c % 8, d), F32)], axis=0)
    mods = _ada_call(c_all, w_ada[0], b_ada[0])
    mods_p = [mods[:batch, i * d:(i + 1) * d].reshape(batch, 1, d) for i in range(N_MOD)]
    mods_s = [jnp.repeat(mods[batch:n_c, i * d:(i + 1) * d], ntok, axis=0).reshape(1, m_s, d) for i in range(N_MOD)]

    def stream(x, mod, w1, tm_f, tm_p, tiles_f, tiles_p, decode):
        sh1, sc1, g1, sh2, sc2, g2, sh3, sc3, g3 = mod
        h = _ffn_call(x, sh1, sc1, g1, norm_ffn1[0], *w1, tm=tm_f, tiles_per_group=tiles_f, emit_bf16=decode)
        if decode:
            h, *w1 = h
        proj = _inproj_call(h, sh2, sc2, norm_mix[0], w_in_t, q_norm[0], k_norm[0],
                            idx_k_norm[0], v_norm[0], tm=tm_p, tiles_per_group=tiles_p, decode=decode)
        return h, proj, (g2, sh3, sc3, g3), w1

    def finish(h, out_a, u, vb, rest, w2, ws, bs_t, tm_f, tm_p, tiles_f, tiles_p, unit, decode):
        g2, sh3, sc3, g3 = rest
        h2 = _outproj_call(out_a, u, vb, h, g2, ws, bs_t, out_norm_a[0], out_norm_b[0], wo,
                           tm=tm_p, tiles_per_group=tiles_p, unit=unit)
        return _ffn_call(h2, sh3, sc3, g3, norm_ffn2[0], *w2, tm=tm_f, tiles_per_group=tiles_f,
                         emit_bf16=decode)

    xs = x_sample.reshape(m_s, d)
    h_s, proj_s, rest_s, w1_bf16 = stream(xs, mods_s, (w1_gate[0], w1_up[0], w1_down[0]), m_s, m_s, 1, 1, True)
    q_s, k_s, kb_s, v_s, u_s, vb_s = proj_s
    qi_s, ki_s, wi_s = _idxproj_call(h_s, mods_s[3], mods_s[4], norm_mix[0], w_idx, idx_k_norm[0])
    q_rows = q_s.reshape(nseq, ntok, N_HEADS, HEAD_DIM).transpose(0, 2, 1, 3).reshape(
        nseq, N_HEADS * ntok, HEAD_DIM)
    qi2d = qi_s.reshape(nseq, ntok, IDX_HEADS, IDX_DIM).transpose(0, 2, 1, 3).reshape(
        nseq, IDX_HEADS * ntok, IDX_DIM)
    wcol = wi_s.reshape(nseq, ntok, IDX_HEADS).transpose(0, 2, 1).reshape(nseq, IDX_HEADS * ntok, 1)

    def pad_rows(a, rows):
        a = a.reshape(nseq, -1, a.shape[-1])
        return jnp.pad(a, ((0, 0), (0, rows - a.shape[1]), (0, 0)))

    bias = _select_decode_call(page_table, qi2d, wcol, pad_rows(ki_s, PAGE_SIZE).transpose(0, 2, 1),
                               cache_kidx[0].transpose(0, 2, 1), pages_per_step=math.gcd(n_pages, 16))
    slope_col = jnp.repeat(_alibi_slopes(), ntok).reshape(N_HEADS * ntok, 1)
    out_a_s = _attn_decode_call(page_table, q_rows, slope_col, bias,
                                pad_rows(kb_s.reshape(m_s * N_HEADS, HEAD_DIM), PAGE_ROWS),
                                pad_rows(v_s.astype(BF16).reshape(m_s * N_HEADS, HEAD_DIM), PAGE_ROWS),
                                cache_k.reshape(n_pool, PAGE_ROWS, HEAD_DIM),
                                cache_v.reshape(n_pool, PAGE_ROWS, HEAD_DIM),
                                pages_per_step=math.gcd(n_pages, 8))
    out_a_s = out_a_s.reshape(nseq, N_HEADS, ntok, HEAD_DIM).transpose(0, 2, 1, 3).reshape(m_s, w_attn)
    ws_s = jnp.einsum("ab,gts->gatbs", jnp.eye(nseq, dtype=F32), w_spatial[0][:, :ntok, :ntok]).reshape(
        N_GROUPS, m_s, m_s)
    bs_s = jnp.tile(b_spatial[0][:, :ntok].T, (nseq, 1))
    y_s, *w2_bf16 = finish(h_s, out_a_s, u_s, vb_s, rest_s, (w2_gate[0], w2_up[0], w2_down[0]), ws_s, bs_s,
                           m_s, m_s, 1, 1, m_s, True)

    xp = x_prompt.reshape(batch * seq, d)
    tf_p, tp_p = seq // tm_ffn, seq // tq
    h_p, proj_p, rest_p, _ = stream(xp, mods_p, w1_bf16, tm_ffn, tq, tf_p, tp_p, False)
    qT, qiT, wT, vT, k_p, kb_p, v_p, ki_p, kib_p, u_p, vb_p = proj_p
    out_a_p = _attn_prompt_call(qT, qiT, wT, kb_p, vT, kib_p, batch=batch, seq=seq, tq=tq)
    y_p = finish(h_p, out_a_p, u_p, vb_p, rest_p, w2_bf16, w_spatial[0], b_spatial[0].T,
                 tm_ffn, tq, tf_p, tp_p, CHUNK, False)

    return (
        y_p.reshape(batch, seq, d),
        y_s.reshape(nseq, ntok, d),
        k_p.reshape(1, batch, seq, N_HEADS, HEAD_DIM),
        v_p.reshape(1, batch, seq, N_HEADS, HEAD_DIM),
        ki_p.reshape(1, batch, seq, IDX_DIM),
        k_s.reshape(1, nseq, ntok, N_HEADS, HEAD_DIM),
        v_s.reshape(1, nseq, ntok, N_HEADS, HEAD_DIM),
        ki_s.reshape(1, nseq, ntok, IDX_DIM),
        vb_s.reshape(1, nseq, ntok, w_g),
    )


def kernel(x_prompt, x_sample, cache_k, cache_v, cache_kidx, page_table, c_prompt, c_sample, w_ada, b_ada, norm_ffn1, w1_gate, w1_up, w1_down, norm_mix, w_in, q_norm, k_norm, idx_k_norm, w_spatial, b_spatial, v_norm, out_norm_a, out_norm_b, w_out, norm_ffn2, w2_gate, w2_up, w2_down):
    return _forward(x_prompt, x_sample, cache_k, cache_v, cache_kidx, page_table, c_prompt, c_sample,
                    w_ada, b_ada, norm_ffn1, w1_gate, w1_up, w1_down, norm_mix, w_in, q_norm, k_norm,
                    idx_k_norm, w_spatial, b_spatial, v_norm, out_norm_a, out_norm_b, w_out, norm_ffn2,
                    w2_gate, w2_up, w2_down)
```

```python
import functools
import math

import jax
import jax.numpy as jnp
from jax import lax
from jax.experimental import pallas as pl
from jax.experimental.pallas import tpu as pltpu

F32 = jnp.float32
BF16 = jnp.bfloat16
I32 = jnp.int32

EPS = 1e-6
HEAD_DIM = 128
N_HEADS = 8
N_GROUPS = 8
GROUP_DIM = 128
CHUNK = 128
IDX_HEADS = 16
IDX_DIM = 64
TOPK_MAX = 256
PAGE_SIZE = 128
N_MOD = 9
SUBLANES = 8
PAGE_ROWS = PAGE_SIZE * N_HEADS

INT_MIN = -(2 ** 31)
INT_MAX = 2 ** 31 - 1
F32_MIN = float(jnp.finfo(jnp.float32).min)
LOG2E = math.log2(math.e)
QK_SCALE = (HEAD_DIM ** -0.5) * LOG2E

VMEM_LIMIT_BYTES = 60000 * 1024
FFN_VMEM_LIMIT_BYTES = 62 * 1024 * 1024

NT_DIMS = (((1,), (1,)), ((), ()))


def _compiler_params(semantics, vmem_limit_bytes=VMEM_LIMIT_BYTES):
    return pltpu.CompilerParams(dimension_semantics=semantics, vmem_limit_bytes=vmem_limit_bytes)


def _modulate(x, gain, shift, scale):
    ms = jnp.mean(x * x, axis=-1, keepdims=True)
    y = x * lax.rsqrt(ms + EPS)
    return (y * gain) * (1.0 + scale) + shift


def _rms_rows(x, gain):
    ms = jnp.mean(x * x, axis=-1, keepdims=True)
    return (x * lax.rsqrt(ms + EPS)) * gain


def _div_pow2(x, n):
    assert n & (n - 1) == 0
    return lax.shift_right_logical(x, jnp.int32(n.bit_length() - 1))


def _mod_pow2(x, n):
    assert n & (n - 1) == 0
    return x & jnp.int32(n - 1)


LANES = 128


def _fold_lane_tiles(x, op):
    return functools.reduce(op, [x[:, i * LANES:(i + 1) * LANES] for i in range(x.shape[1] // LANES)])


def _rms_cols(x, gain_col):
    ms = jnp.mean(x * x, axis=0, keepdims=True)
    return (x * lax.rsqrt(ms + EPS)) * gain_col


def _ada_kernel(c_ref, w_ref, b_ref, o_ref):
    a = jax.nn.silu(c_ref[...]).astype(BF16)
    o_ref[...] = jnp.dot(a, w_ref[...].astype(BF16), preferred_element_type=F32) + b_ref[...]


def _ada_call(c_all, w_ada, b_ada):
    rows, d = c_all.shape
    n = w_ada.shape[1]
    tn = 1024
    return pl.pallas_call(
        _ada_kernel,
        grid=(n // tn,),
        in_specs=[
            pl.BlockSpec((rows, d), lambda j: (0, 0)),
            pl.BlockSpec((d, tn), lambda j: (0, j)),
            pl.BlockSpec((1, tn), lambda j: (0, j)),
        ],
        out_specs=pl.BlockSpec((rows, tn), lambda j: (0, j)),
        out_shape=jax.ShapeDtypeStruct((rows, n), F32),
        compiler_params=_compiler_params(("parallel",)),
        name="ada_proj",
    )(c_all, w_ada, b_ada.reshape(1, n))


def _ffn_kernel(x_ref, sh_ref, sc_ref, g_ref, nw_ref, wg_ref, wu_ref, wd_ref, o_ref, *rest, emit_bf16):
    if emit_bf16:
        wgb_ref, wub_ref, wdb_ref, xn_ref = rest
    else:
        (xn_ref,) = rest
    n = pl.program_id(1)

    @pl.when(n == 0)
    def _():
        xn_ref[...] = _modulate(x_ref[...], nw_ref[...], sh_ref[...], sc_ref[...]).astype(BF16)
        o_ref[...] = jnp.zeros_like(o_ref)

    wg, wu, wd = wg_ref[...].astype(BF16), wu_ref[...].astype(BF16), wd_ref[...].astype(BF16)
    if emit_bf16:
        wgb_ref[...] = wg
        wub_ref[...] = wu
        wdb_ref[...] = wd
    xn = xn_ref[...]
    gate = jnp.dot(xn, wg, preferred_element_type=F32)
    up = jnp.dot(xn, wu, preferred_element_type=F32)
    hmid = (jax.nn.silu(gate) * up).astype(BF16)
    o_ref[...] += jnp.dot(hmid, wd, preferred_element_type=F32)

    @pl.when(n == pl.num_programs(1) - 1)
    def _():
        o_ref[...] = x_ref[...] + (0.5 * g_ref[...]) * o_ref[...]


def _mod_spec(mod, tiles_per_group):
    _, r, d = mod.shape
    return pl.BlockSpec((None, r, d), lambda i, *_: (i // tiles_per_group, 0, 0))


def _ffn_call(x, shift, scale, gate, norm_w, wg, wu, wd, *, tm, tiles_per_group, emit_bf16=False):
    m, d = x.shape
    dff = wg.shape[1]
    tn = 512
    assert not emit_bf16 or m == tm
    col = lambda: pl.BlockSpec((d, tn), lambda i, n: (0, n))
    row = lambda: pl.BlockSpec((tn, d), lambda i, n: (n, 0))
    out_specs = [pl.BlockSpec((tm, d), lambda i, n: (i, 0))]
    out_shape = [jax.ShapeDtypeStruct((m, d), F32)]
    if emit_bf16:
        out_specs += [col(), col(), row()]
        out_shape += [jax.ShapeDtypeStruct(w.shape, BF16) for w in (wg, wu, wd)]
    outs = pl.pallas_call(
        functools.partial(_ffn_kernel, emit_bf16=emit_bf16),
        grid=(m // tm, dff // tn),
        in_specs=[
            pl.BlockSpec((tm, d), lambda i, n: (i, 0)),
            _mod_spec(shift, tiles_per_group),
            _mod_spec(scale, tiles_per_group),
            _mod_spec(gate, tiles_per_group),
            pl.BlockSpec((1, d), lambda i, n: (0, 0)),
            col(), col(), row(),
        ],
        out_specs=out_specs,
        out_shape=out_shape,
        scratch_shapes=[pltpu.VMEM((tm, d), BF16)],
        compiler_params=_compiler_params(("parallel", "arbitrary"), FFN_VMEM_LIMIT_BYTES),
        name="ffn",
    )(x, shift, scale, gate, norm_w.reshape(1, d), wg, wu, wd)
    return outs if emit_bf16 else outs[0]


W_ATTN = N_HEADS * HEAD_DIM
W_IDX = IDX_HEADS * IDX_DIM
W_GATE = N_GROUPS * GROUP_DIM
OFF_Q, OFF_K, OFF_V = 0, W_ATTN, 2 * W_ATTN
OFF_QI = 3 * W_ATTN
OFF_KIW = OFF_QI + W_IDX
OFF_U = OFF_KIW + IDX_DIM + IDX_HEADS
OFF_VB = OFF_U + W_GATE
D_IN = OFF_VB + W_GATE
assert OFF_U % 16 == 0 and OFF_VB % 16 == 0


def _head_rows(h, tm):
    return pl.ds(h, tm, stride=N_HEADS)


def _inproj_kernel(h_ref, sh_ref, sc_ref, nw_ref, w_ref, qn_ref, kn_ref, ikn_ref, vn_ref,
                   qT_ref, qiT_ref, wT_ref, vT_ref, k_ref, kb_ref, v_ref, ki_ref, kib_ref, u_ref, vb_ref):
    tm = h_ref.shape[0]
    a = _modulate(h_ref[...], nw_ref[...], sh_ref[...], sc_ref[...]).astype(BF16)

    def proj(off, rows):
        return lax.dot_general(w_ref[off:off + rows, :], a, NT_DIMS, preferred_element_type=F32)

    q_t = proj(OFF_Q, W_ATTN)
    k_t = proj(OFF_K, W_ATTN)
    v_t = proj(OFF_V, W_ATTN)
    for h in range(N_HEADS):
        rows = slice(h * HEAD_DIM, (h + 1) * HEAD_DIM)
        qT_ref[rows, :] = (_rms_cols(q_t[rows, :], qn_ref[...]) * QK_SCALE).astype(BF16)
        kh = _rms_cols(k_t[rows, :], kn_ref[...]).T
        kb_ref[:, rows] = kh.astype(BF16)
        k_ref[_head_rows(h, tm), :] = kh
        v_ref[_head_rows(h, tm), :] = v_t[rows, :].T
    vT_ref[...] = v_t.astype(BF16)
    qiT_ref[...] = (proj(OFF_QI, W_IDX) * (IDX_DIM ** -0.5)).astype(BF16)
    tail = proj(OFF_KIW, LANES)
    wT_ref[...] = tail[IDX_DIM:IDX_DIM + IDX_HEADS, :] * (IDX_HEADS ** -0.5)
    ki_t = _rms_cols(tail[0:IDX_DIM, :], ikn_ref[...])
    ki = jnp.concatenate([ki_t, tail[IDX_DIM:, :]], axis=0).T[:, 0:IDX_DIM]
    ki_ref[...] = ki
    kib_ref[...] = ki.astype(BF16)
    u_t = proj(OFF_U, W_GATE)
    vb_t = proj(OFF_VB, W_GATE)
    for g in range(N_GROUPS):
        rows = slice(g * GROUP_DIM, (g + 1) * GROUP_DIM)
        u_ref[:, rows] = jax.nn.gelu(u_t[rows, :]).T
        vb_ref[:, rows] = _rms_cols(jax.nn.gelu(vb_t[rows, :]), vn_ref[:, g:g + 1]).T


def _inproj_decode_kernel(h_ref, sh_ref, sc_ref, nw_ref, w_ref, qn_ref, kn_ref, vn_ref,
                          q_ref, k_ref, kb_ref, v_ref, u_ref, vb_ref):
    tm = h_ref.shape[0]
    a = _modulate(h_ref[...], nw_ref[...], sh_ref[...], sc_ref[...]).astype(BF16)

    def proj(off, width):
        return lax.dot_general(a, w_ref[off:off + width, :], NT_DIMS, preferred_element_type=F32)

    q = proj(OFF_Q, W_ATTN)
    k = proj(OFF_K, W_ATTN)
    v = proj(OFF_V, W_ATTN)
    for h in range(N_HEADS):
        cols = slice(h * HEAD_DIM, (h + 1) * HEAD_DIM)
        q_ref[:, cols] = (_rms_rows(q[:, cols], qn_ref[...]) * QK_SCALE).astype(BF16)
        kh = _rms_rows(k[:, cols], kn_ref[...])
        kb_ref[:, cols] = kh.astype(BF16)
        k_ref[_head_rows(h, tm), :] = kh
        v_ref[_head_rows(h, tm), :] = v[:, cols]
    u_ref[...] = jax.nn.gelu(proj(OFF_U, W_GATE))
    vb = jax.nn.gelu(proj(OFF_VB, W_GATE))
    for g in range(N_GROUPS):
        cols = slice(g * GROUP_DIM, (g + 1) * GROUP_DIM)
        vb_ref[:, cols] = _rms_rows(vb[:, cols], vn_ref[g:g + 1, :])


def _inproj_call(h, shift, scale, norm_w, w_t, q_norm, k_norm, idx_k_norm, v_norm, *, tm, tiles_per_group,
                 decode):
    m, d = h.shape
    nt = m // tm
    row = lambda i: (i, 0)
    fm = lambda i: (i, 0, 0)
    token_major = lambda width, dtype: (jax.ShapeDtypeStruct((m, width), dtype), pl.BlockSpec((tm, width), row))
    feature_major = lambda width, dtype: (jax.ShapeDtypeStruct((nt, width, tm), dtype),
                                          pl.BlockSpec((None, width, tm), fm))
    head_major = lambda: (jax.ShapeDtypeStruct((m * N_HEADS, HEAD_DIM), F32),
                          pl.BlockSpec((tm * N_HEADS, HEAD_DIM), row))
    if decode:
        outs = [token_major(W_ATTN, BF16)]
        gains = [q_norm.reshape(1, HEAD_DIM), k_norm.reshape(1, HEAD_DIM), v_norm]
    else:
        outs = [feature_major(W_ATTN, BF16), feature_major(W_IDX, BF16),
                feature_major(IDX_HEADS, F32), feature_major(W_ATTN, BF16)]
        gains = [q_norm.reshape(HEAD_DIM, 1), k_norm.reshape(HEAD_DIM, 1), idx_k_norm.reshape(IDX_DIM, 1),
                 v_norm.T]
    outs += [head_major(), token_major(W_ATTN, BF16), head_major()]
    if not decode:
        outs += [token_major(IDX_DIM, F32), token_major(IDX_DIM, BF16)]
    outs += [token_major(W_GATE, F32), token_major(W_GATE, F32)]
    const2d = lambda a: pl.BlockSpec(a.shape, lambda i: (0, 0))
    return pl.pallas_call(
        _inproj_decode_kernel if decode else _inproj_kernel,
        grid=(nt,),
        in_specs=[
            pl.BlockSpec((tm, d), row),
            _mod_spec(shift, tiles_per_group),
            _mod_spec(scale, tiles_per_group),
            pl.BlockSpec((1, d), lambda i: (0, 0)),
            const2d(w_t),
        ] + [const2d(g) for g in gains],
        out_specs=[spec for _, spec in outs],
        out_shape=[shape for shape, _ in outs],
        compiler_params=_compiler_params(("parallel",)),
        name="in_proj",
    )(h, shift, scale, norm_w.reshape(1, d), w_t, *gains)


def _dot3(a, b, dims=(((1,), (0,)), ((), ()))):
    a_hi = a.astype(BF16)
    b_hi = b.astype(BF16)
    a_lo = (a - a_hi.astype(F32)).astype(BF16)
    b_lo = (b - b_hi.astype(F32)).astype(BF16)
    dot = functools.partial(lax.dot_general, dimension_numbers=dims, preferred_element_type=F32)
    return dot(a_hi, b_hi) + (dot(a_lo, b_hi) + dot(a_hi, b_lo))


def _idxproj_kernel(h_ref, sh_ref, sc_ref, nw_ref, w_ref, ikn_ref, qi_ref, ki_ref, wi_ref):
    a = _modulate(h_ref[...], nw_ref[...], sh_ref[...], sc_ref[...])
    r = _dot3(a, w_ref[...], NT_DIMS)
    qi_ref[...] = r[:, 0:W_IDX] * (IDX_DIM ** -0.5)
    kiw = r[:, W_IDX:W_IDX + LANES]
    ki_ref[...] = _rms_rows(kiw[:, 0:IDX_DIM], ikn_ref[...])
    wi_ref[...] = kiw[:, IDX_DIM:IDX_DIM + IDX_HEADS] * (IDX_HEADS ** -0.5)


def _idxproj_call(h, shift, scale, norm_w, w_idx, idx_k_norm):
    m, d = h.shape
    full = lambda a: pl.BlockSpec(a.shape, lambda i: (0,) * a.ndim)
    args = (h, shift[0], scale[0], norm_w.reshape(1, d), w_idx, idx_k_norm.reshape(1, IDX_DIM))
    return pl.pallas_call(
        _idxproj_kernel,
        grid=(1,),
        in_specs=[full(a) for a in args],
        out_specs=[pl.BlockSpec((m, W_IDX), lambda i: (0, 0)), pl.BlockSpec((m, IDX_DIM), lambda i: (0, 0)),
                   pl.BlockSpec((m, IDX_HEADS), lambda i: (0, 0))],
        out_shape=[jax.ShapeDtypeStruct((m, W_IDX), F32), jax.ShapeDtypeStruct((m, IDX_DIM), F32),
                   jax.ShapeDtypeStruct((m, IDX_HEADS), F32)],
        compiler_params=_compiler_params(("arbitrary",)),
        name="idx_proj_decode",
    )(*args)


def _key_to_float(key):
    return lax.bitcast_convert_type(jnp.where(key < 0, key ^ jnp.int32(INT_MAX), key), F32)


def _topk_threshold(count_ge, topk, like):
    zero = jnp.zeros_like(like)
    cnt0 = count_ge(_key_to_float(zero))
    nonneg = cnt0 >= topk
    t0 = jnp.where(nonneg, zero, jnp.int32(INT_MIN))
    c0 = jnp.where(nonneg, cnt0, jnp.int32(topk))

    def body(p, carry):
        t, ct = carry
        cand = t | jnp.left_shift(jnp.int32(1), 31 - p)
        cnt = count_ge(_key_to_float(cand))
        take = cnt >= topk
        return jnp.where(take, cand, t), jnp.where(take, cnt, ct)

    t, ct = lax.fori_loop(1, 32, body, (t0, c0))
    return t != jnp.int32(INT_MIN), _key_to_float(t), ct


def _tie_cutoff(count_where, thr, topk, nbits, excess):
    need = topk - count_where(lambda sc, pos: sc > thr)
    j = jnp.zeros_like(need)
    for bit in reversed(range(nbits)):
        cand = j | jnp.int32(1 << bit)
        cnt = count_where(lambda sc, pos, cand=cand: (sc == thr) & (pos < cand))
        j = jnp.where(cnt < need, cand, j)
    return jnp.where(excess, j, jnp.int32(INT_MAX))


def _select(count_where, topk, pos_bits, like):
    has_thr, thr, cnt_thr = _topk_threshold(lambda t: count_where(lambda sc, pos: sc >= t), topk, like)
    excess = (cnt_thr > topk) & has_thr
    cutoff = lax.cond(
        jnp.max(excess.astype(I32)) > 0,
        lambda: _tie_cutoff(count_where, thr, topk, pos_bits, excess),
        lambda: jnp.full(like.shape, INT_MAX, I32),
    )
    return lambda sc, pos: jnp.logical_not(has_thr) | (sc > thr) | ((sc == thr) & (pos <= cutoff))


def _attn_prompt_kernel(qT_ref, qiT_ref, wT_ref, kb_ref, vT_ref, kib_ref, o_ref,
                        sc_ref, acc_ref, m_ref, l_ref, a_ref, s_ref, p_ref, *, tq, topk, pos_bits):
    j = pl.program_id(1)
    nchunks = j + 1
    groups = tq // SUBLANES
    t_pos = j * tq + lax.broadcasted_iota(I32, (1, tq), 1)
    s_off = lax.broadcasted_iota(I32, (tq, 1), 0)

    def chunk_start(c):
        return pl.multiple_of(c * tq, tq)

    def score_body(c, carry):
        off = chunk_start(c)
        kic = kib_ref[pl.ds(off, tq), :]
        sc = jnp.zeros((tq, tq), F32)
        for h in range(IDX_HEADS):
            d = jnp.dot(kic, qiT_ref[h * IDX_DIM:(h + 1) * IDX_DIM, :], preferred_element_type=F32)
            sc = sc + jnp.maximum(d, 0.0) * wT_ref[h:h + 1, :]
        sc_ref[pl.ds(off, tq), :] = jnp.where((off + s_off) <= t_pos, sc, -jnp.inf)
        return carry

    lax.fori_loop(0, nchunks, score_body, 0)

    def count_where(pred):
        def body(c, cnt):
            off = chunk_start(c)
            hit = pred(sc_ref[pl.ds(off, tq), :], off + s_off).reshape(groups, SUBLANES, tq)
            accs = [cnt] + [jnp.zeros_like(cnt)] * 3
            for g in range(groups):
                accs[g % 4] = jnp.where(hit[g], accs[g % 4] + 1, accs[g % 4])
            return (accs[0] + accs[1]) + (accs[2] + accs[3])

        cnt8 = lax.fori_loop(0, nchunks, body, jnp.zeros((SUBLANES, tq), I32))
        return jnp.sum(cnt8, axis=0, keepdims=True)

    selected = _select(count_where, topk, pos_bits, t_pos)

    m_ref[...] = jnp.full(m_ref.shape, F32_MIN, F32)
    l_ref[...] = jnp.zeros_like(l_ref)
    acc_ref[...] = jnp.zeros_like(acc_ref)

    def att_body(c, carry):
        off = chunk_start(c)
        pos = off + s_off
        sel = selected(sc_ref[pl.ds(off, tq), :], pos) & (pos <= t_pos)
        dmc = jnp.where(sel, (t_pos - pos).astype(F32) * LOG2E, jnp.inf)
        for h in range(N_HEADS):
            rows = slice(h * HEAD_DIM, (h + 1) * HEAD_DIM)
            qk = jnp.dot(kb_ref[pl.ds(off, tq), rows], qT_ref[rows, :], preferred_element_type=F32)
            s = qk - (2.0 ** -(h + 1)) * dmc
            s_ref[h] = s
            cm = jnp.max(s.reshape(groups, SUBLANES, tq), axis=0)
            m_old = m_ref[h]
            m_new = jnp.maximum(m_old, jnp.broadcast_to(jnp.max(cm, axis=0, keepdims=True), cm.shape))
            a_ref[h] = jnp.exp2(m_old - m_new)
            m_ref[h] = m_new
        for h in range(N_HEADS):
            p = jnp.exp2(s_ref[h].reshape(groups, SUBLANES, tq) - m_ref[h][None])
            l_ref[h] = a_ref[h] * l_ref[h] + jnp.sum(p, axis=0)
            p_ref[h] = p.reshape(tq, tq).astype(BF16)
        for h in range(N_HEADS):
            rows = slice(h * HEAD_DIM, (h + 1) * HEAD_DIM)
            pv = jnp.dot(vT_ref[c, rows, :], p_ref[h], preferred_element_type=F32)
            acc = acc_ref[rows, :].reshape(HEAD_DIM // SUBLANES, SUBLANES, tq) * a_ref[h][None]
            acc_ref[rows, :] = acc.reshape(HEAD_DIM, tq) + pv
        return carry

    lax.fori_loop(0, nchunks, att_body, 0)

    for h in range(N_HEADS):
        rows = slice(h * HEAD_DIM, (h + 1) * HEAD_DIM)
        acc_ref[rows, :] = acc_ref[rows, :] / jnp.sum(l_ref[h], axis=0, keepdims=True)
    o_ref[...] = acc_ref[...].T


def _attn_prompt_call(qT, qiT, wT, kb, vT, kib, *, batch, seq, tq):
    w_attn = N_HEADS * HEAD_DIM
    w_idx = IDX_HEADS * IDX_DIM
    nq = seq // tq
    topk = min(TOPK_MAX, seq // 4)
    qblk = lambda b, j: (b * nq + j, 0, 0)
    kernel = functools.partial(_attn_prompt_kernel, tq=tq, topk=topk, pos_bits=max(1, (seq - 1).bit_length()))
    return pl.pallas_call(
        kernel,
        grid=(batch, nq),
        in_specs=[
            pl.BlockSpec((None, w_attn, tq), qblk),
            pl.BlockSpec((None, w_idx, tq), qblk),
            pl.BlockSpec((None, IDX_HEADS, tq), qblk),
            pl.BlockSpec((seq, w_attn), lambda b, j: (b, 0), pipeline_mode=pl.Buffered(1)),
            pl.BlockSpec((nq, w_attn, tq), lambda b, j: (b, 0, 0), pipeline_mode=pl.Buffered(1)),
            pl.BlockSpec((seq, IDX_DIM), lambda b, j: (b, 0), pipeline_mode=pl.Buffered(1)),
        ],
        out_specs=pl.BlockSpec((tq, w_attn), lambda b, j: (b * nq + j, 0)),
        out_shape=jax.ShapeDtypeStruct((batch * seq, w_attn), F32),
        scratch_shapes=[
            pltpu.VMEM((seq, tq), F32),
            pltpu.VMEM((w_attn, tq), F32),
            pltpu.VMEM((N_HEADS, SUBLANES, tq), F32),
            pltpu.VMEM((N_HEADS, SUBLANES, tq), F32),
            pltpu.VMEM((N_HEADS, SUBLANES, tq), F32),
            pltpu.VMEM((N_HEADS, tq, tq), F32),
            pltpu.VMEM((N_HEADS, tq, tq), BF16),
        ],
        compiler_params=_compiler_params(("parallel", "arbitrary")),
        name="attn_prompt",
    )(qT, qiT, wT, kb, vT, kib)


def _outproj_kernel(oa_ref, u_ref, vb_ref, h_ref, g_ref, ws_ref, bs_ref, na_ref, nb_ref, wo_ref, o_ref, *, unit):
    tm = oa_ref.shape[0]
    w_attn = N_HEADS * HEAD_DIM
    rows_i = lax.broadcasted_iota(I32, (unit, unit), 0)
    cols_i = lax.broadcasted_iota(I32, (unit, unit), 1)
    causal = rows_i >= cols_i
    ob_parts = []
    for g in range(N_GROUPS):
        cols = slice(g * GROUP_DIM, (g + 1) * GROUP_DIM)
        wm = jnp.where(causal, ws_ref[g], 0.0).astype(BF16)
        bias = bs_ref[:, g:g + 1]
        mixed = []
        for r in range(tm // unit):
            rws = slice(r * unit, (r + 1) * unit)
            mixed.append(jnp.dot(wm, vb_ref[rws, cols].astype(BF16), preferred_element_type=F32) + bias)
        mixed = mixed[0] if len(mixed) == 1 else jnp.concatenate(mixed, axis=0)
        ob_parts.append(u_ref[:, cols] * mixed)
    ob = jnp.concatenate(ob_parts, axis=1)
    za = _rms_rows(oa_ref[...], na_ref[...]).astype(BF16)
    zb = _rms_rows(ob, nb_ref[...]).astype(BF16)
    mix = (jnp.dot(za, wo_ref[0:w_attn, :], preferred_element_type=F32)
           + jnp.dot(zb, wo_ref[w_attn:, :], preferred_element_type=F32))
    o_ref[...] = h_ref[...] + g_ref[...] * mix


def _outproj_call(out_a, u, vb, h, gate, ws, bs_t, norm_a, norm_b, w_out, *, tm, tiles_per_group, unit):
    m, d = h.shape
    w_attn = out_a.shape[1]
    w_g = u.shape[1]
    row = lambda i: (i, 0)
    const = lambda i: (0, 0)
    return pl.pallas_call(
        functools.partial(_outproj_kernel, unit=unit),
        grid=(m // tm,),
        in_specs=[
            pl.BlockSpec((tm, w_attn), row),
            pl.BlockSpec((tm, w_g), row),
            pl.BlockSpec((tm, w_g), row),
            pl.BlockSpec((tm, d), row),
            _mod_spec(gate, tiles_per_group),
            pl.BlockSpec(ws.shape, lambda i: (0, 0, 0)),
            pl.BlockSpec(bs_t.shape, const),
            pl.BlockSpec((1, w_attn), const),
            pl.BlockSpec((1, w_g), const),
            pl.BlockSpec(w_out.shape, const),
        ],
        out_specs=pl.BlockSpec((tm, d), row),
        out_shape=jax.ShapeDtypeStruct((m, d), F32),
        compiler_params=_compiler_params(("parallel",)),
        name="out_proj",
    )(out_a, u, vb, h, gate, ws, bs_t, norm_a.reshape(1, w_attn), norm_b.reshape(1, w_g), w_out)


def _select_decode_kernel(pt_ref, qi_ref, w_ref, kin_ref, *rest, pages_per_step, n_pages, topk, pos_bits):
    page_refs = rest[:pages_per_step]
    bias_ref = rest[pages_per_step]
    sc_ref = rest[pages_per_step + 1]
    p = pl.program_id(1)
    qi = qi_ref[...]
    w = w_ref[...]

    def page_scores(keys_t):
        d = _dot3(qi, keys_t)
        r = jnp.maximum(d, 0.0) * w
        return jnp.sum(r.reshape(IDX_HEADS, SUBLANES, PAGE_SIZE), axis=0)

    for i in range(pages_per_step):
        sc_ref[p * pages_per_step + i] = page_scores(page_refs[i][...])

    @pl.when(p == pl.num_programs(1) - 1)
    def _():
        shape = (n_pages + 1, SUBLANES, PAGE_SIZE)
        tok = lax.broadcasted_iota(I32, (SUBLANES, PAGE_SIZE), 0)
        lane = lax.broadcasted_iota(I32, (SUBLANES, PAGE_SIZE), 1)
        sc_ref[n_pages] = jnp.where(lane <= tok, page_scores(kin_ref[...]), -jnp.inf)
        pos = lax.broadcasted_iota(I32, shape, 0) * PAGE_SIZE + lax.broadcasted_iota(I32, shape, 2)
        q_pos = n_pages * PAGE_SIZE + lax.broadcasted_iota(I32, shape, 1)

        def count_where(pred):
            hit = pred(sc_ref[...], pos).astype(I32)
            return jnp.sum(jnp.sum(hit, axis=0), axis=-1, keepdims=True)

        selected = _select(count_where, topk, pos_bits, jnp.zeros((SUBLANES, 1), I32))
        sel = selected(sc_ref[...], pos) & (pos <= q_pos)
        sel2d = jnp.where(sel, 1.0, 0.0).astype(BF16).reshape((n_pages + 1) * SUBLANES, PAGE_SIZE)
        expand = (_div_pow2(lax.broadcasted_iota(I32, (PAGE_SIZE, PAGE_ROWS), 1), N_HEADS)
                  == lax.broadcasted_iota(I32, (PAGE_SIZE, PAGE_ROWS), 0))
        rep = jnp.dot(sel2d, jnp.where(expand, 1.0, 0.0).astype(BF16), preferred_element_type=F32)
        bias_ref[...] = jnp.where(rep > 0.5, 0.0, -jnp.inf).reshape(n_pages + 1, SUBLANES, PAGE_ROWS)


def _select_decode_call(page_table, qi2d, wcol, ki_new, cache_kidx, *, pages_per_step):
    nseq, n_pages = page_table.shape
    topk = min(TOPK_MAX, (n_pages * PAGE_SIZE + SUBLANES) // 4)
    pos_bits = max(1, ((n_pages + 1) * PAGE_SIZE - 1).bit_length())
    per_seq = lambda b, p, pt: (b, 0, 0)

    def page_spec(i):
        return pl.BlockSpec((None, IDX_DIM, PAGE_SIZE),
                            lambda b, p, pt, i=i: (pt[b, p * pages_per_step + i], 0, 0))

    kernel = functools.partial(_select_decode_kernel, pages_per_step=pages_per_step, n_pages=n_pages,
                               topk=topk, pos_bits=pos_bits)
    grid_spec = pltpu.PrefetchScalarGridSpec(
        num_scalar_prefetch=1,
        grid=(nseq, n_pages // pages_per_step),
        in_specs=[
            pl.BlockSpec((None, IDX_HEADS * SUBLANES, IDX_DIM), per_seq),
            pl.BlockSpec((None, IDX_HEADS * SUBLANES, 1), per_seq),
            pl.BlockSpec((None, IDX_DIM, PAGE_SIZE), per_seq),
        ] + [page_spec(i) for i in range(pages_per_step)],
        out_specs=pl.BlockSpec((None, n_pages + 1, SUBLANES, PAGE_ROWS), lambda b, p, pt: (b, 0, 0, 0)),
        scratch_shapes=[pltpu.VMEM((n_pages + 1, SUBLANES, PAGE_SIZE), F32)],
    )
    return pl.pallas_call(
        kernel,
        grid_spec=grid_spec,
        out_shape=jax.ShapeDtypeStruct((nseq, n_pages + 1, SUBLANES, PAGE_ROWS), F32),
        compiler_params=_compiler_params(("parallel", "arbitrary")),
        name="select_decode",
    )(page_table, qi2d, wcol, ki_new, *([cache_kidx] * pages_per_step))


def _attn_decode_kernel(pt_ref, q_ref, slope_ref, bias_ref, kn_ref, vn_ref, *rest, pages_per_step, n_pages):
    k_refs = rest[:pages_per_step]
    v_refs = rest[pages_per_step:2 * pages_per_step]
    o_ref, a_ref, acc_ref, m_ref, l_ref = rest[2 * pages_per_step:]
    p = pl.program_id(1)
    nrow = N_HEADS * SUBLANES
    slope = slope_ref[...] * LOG2E

    @pl.when(p == 0)
    def _():
        m_ref[...] = jnp.full(m_ref.shape, F32_MIN, F32)
        l_ref[...] = jnp.zeros_like(l_ref)
        acc_ref[...] = jnp.zeros_like(acc_ref)
        row = lax.broadcasted_iota(I32, (nrow, PAGE_ROWS), 0)
        col = lax.broadcasted_iota(I32, (nrow, PAGE_ROWS), 1)
        same_head = _mod_pow2(col, N_HEADS) == _div_pow2(row, SUBLANES)
        tok_minus_key = (_mod_pow2(row, SUBLANES) - _div_pow2(col, N_HEADS)).astype(F32)
        a_ref[...] = jnp.where(same_head, -slope * tok_minus_key, -jnp.inf)

    def logits(keys_bf16, page):
        page_dist = jnp.full((nrow, 1), (n_pages - page) * PAGE_SIZE, I32).astype(F32)
        s = lax.dot_general(q_ref[...], keys_bf16, NT_DIMS, preferred_element_type=F32)
        return s + a_ref[...] - slope * page_dist + jnp.concatenate([bias_ref[page]] * N_HEADS, axis=0)

    def update(s_pages, v_pages):
        lanes_max = functools.reduce(jnp.maximum, [_fold_lane_tiles(s, jnp.maximum) for s in s_pages])
        m_old = m_ref[...]
        m_new = jnp.maximum(m_old, jnp.max(lanes_max, axis=-1, keepdims=True))
        alpha = jnp.exp2(m_old - m_new)
        lanes_sum = jnp.zeros((nrow, PAGE_SIZE), F32)
        acc = alpha * acc_ref[...]
        for s, vals in zip(s_pages, v_pages):
            pr = jnp.exp2(s - m_new)
            lanes_sum = lanes_sum + _fold_lane_tiles(pr, jnp.add)
            acc = acc + jnp.dot(pr.astype(BF16), vals, preferred_element_type=F32)
        l_ref[...] = alpha * l_ref[...] + jnp.sum(lanes_sum, axis=-1, keepdims=True)
        acc_ref[...] = acc
        m_ref[...] = m_new

    update([logits(k_refs[i][...].astype(BF16), p * pages_per_step + i) for i in range(pages_per_step)],
           [v_refs[i][...].astype(BF16) for i in range(pages_per_step)])

    @pl.when(p == pl.num_programs(1) - 1)
    def _():
        update([logits(kn_ref[...], n_pages)], [vn_ref[...]])
        o_ref[...] = acc_ref[...] / l_ref[...]


def _attn_decode_call(page_table, q_rows, slope_col, bias, k_new, v_new, cache_k, cache_v, *, pages_per_step):
    nseq, n_pages = page_table.shape
    nrow = N_HEADS * SUBLANES
    per_seq = lambda b, p, pt: (b, 0, 0)

    def page_spec(i):
        return pl.BlockSpec((None, PAGE_ROWS, HEAD_DIM),
                            lambda b, p, pt, i=i: (pt[b, p * pages_per_step + i], 0, 0))

    kernel = functools.partial(_attn_decode_kernel, pages_per_step=pages_per_step, n_pages=n_pages)
    grid_spec = pltpu.PrefetchScalarGridSpec(
        num_scalar_prefetch=1,
        grid=(nseq, n_pages // pages_per_step),
        in_specs=[
            pl.BlockSpec((None, nrow, HEAD_DIM), per_seq),
            pl.BlockSpec((nrow, 1), lambda b, p, pt: (0, 0)),
            pl.BlockSpec((None, n_pages + 1, SUBLANES, PAGE_ROWS), lambda b, p, pt: (b, 0, 0, 0)),
            pl.BlockSpec((None, PAGE_ROWS, HEAD_DIM), per_seq),
            pl.BlockSpec((None, PAGE_ROWS, HEAD_DIM), per_seq),
        ] + [page_spec(i) for i in range(pages_per_step)] + [page_spec(i) for i in range(pages_per_step)],
        out_specs=pl.BlockSpec((None, nrow, HEAD_DIM), per_seq),
        scratch_shapes=[
            pltpu.VMEM((nrow, PAGE_ROWS), F32),
            pltpu.VMEM((nrow, HEAD_DIM), F32),
            pltpu.VMEM((nrow, 1), F32),
            pltpu.VMEM((nrow, 1), F32),
        ],
    )
    return pl.pallas_call(
        kernel,
        grid_spec=grid_spec,
        out_shape=jax.ShapeDtypeStruct((nseq, nrow, HEAD_DIM), F32),
        compiler_params=_compiler_params(("parallel", "arbitrary")),
        name="attn_decode",
    )(page_table, q_rows, slope_col, bias, k_new, v_new,
      *([cache_k] * pages_per_step), *([cache_v] * pages_per_step))


def _pack_in_weights(w_in):
    assert w_in.shape[1] == D_IN
    w_t = w_in.T
    return w_t.astype(BF16), w_t[OFF_QI:OFF_KIW + LANES]


def _alibi_slopes():
    return jnp.exp2(-8.0 * jnp.arange(1, N_HEADS + 1, dtype=F32) / N_HEADS)


def _forward(x_prompt, x_sample, cache_k, cache_v, cache_kidx, page_table, c_prompt, c_sample,
             w_ada, b_ada, norm_ffn1, w1_gate, w1_up, w1_down, norm_mix, w_in, q_norm, k_norm,
             idx_k_norm, w_spatial, b_spatial, v_norm, out_norm_a, out_norm_b, w_out, norm_ffn2,
             w2_gate, w2_up, w2_down):
    batch, seq, d = x_prompt.shape
    nseq, ntok, _ = x_sample.shape
    n_pages = page_table.shape[1]
    n_pool = cache_k.shape[1]
    w_attn = N_HEADS * HEAD_DIM
    w_g = N_GROUPS * GROUP_DIM
    assert ntok == SUBLANES and seq % 256 == 0 and cache_k.shape[0] == 1
    tq = 256
    tm_ffn = next(t for t in (1024, 512, 256) if seq % t == 0)
    m_s = nseq * ntok

    w_in_t, w_idx = _pack_in_weights(w_in[0])
    wo = w_out[0].astype(BF16)

    n_c = batch + nseq
    c_all = jnp.concatenate([c_prompt, c_sample, jnp.zeros((-n_c % 8, d), F32)], axis=0)
    mods = _ada_call(c_all, w_ada[0], b_ada[0])
    mods_p = [mods[:batch, i * d:(i + 1) * d].reshape(batch, 1, d) for i in range(N_MOD)]
    mods_s = [jnp.repeat(mods[batch:n_c, i * d:(i + 1) * d], ntok, axis=0).reshape(1, m_s, d) for i in range(N_MOD)]

    def stream(x, mod, w1, tm_f, tm_p, tiles_f, tiles_p, decode):
        sh1, sc1, g1, sh2, sc2, g2, sh3, sc3, g3 = mod
        h = _ffn_call(x, sh1, sc1, g1, norm_ffn1[0], *w1, tm=tm_f, tiles_per_group=tiles_f, emit_bf16=decode)
        if decode:
            h, *w1 = h
        proj = _inproj_call(h, sh2, sc2, norm_mix[0], w_in_t, q_norm[0], k_norm[0],
                            idx_k_norm[0], v_norm[0], tm=tm_p, tiles_per_group=tiles_p, decode=decode)
        return h, proj, (g2, sh3, sc3, g3), w1

    def finish(h, out_a, u, vb, rest, w2, ws, bs_t, tm_f, tm_p, tiles_f, tiles_p, unit, decode):
        g2, sh3, sc3, g3 = rest
        h2 = _outproj_call(out_a, u, vb, h, g2, ws, bs_t, out_norm_a[0], out_norm_b[0], wo,
                           tm=tm_p, tiles_per_group=tiles_p, unit=unit)
        return _ffn_call(h2, sh3, sc3, g3, norm_ffn2[0], *w2, tm=tm_f, tiles_per_group=tiles_f,
                         emit_bf16=decode)

    xs = x_sample.reshape(m_s, d)
    h_s, proj_s, rest_s, w1_bf16 = stream(xs, mods_s, (w1_gate[0], w1_up[0], w1_down[0]), m_s, m_s, 1, 1, True)
    q_s, k_s, kb_s, v_s, u_s, vb_s = proj_s
    qi_s, ki_s, wi_s = _idxproj_call(h_s, mods_s[3], mods_s[4], norm_mix[0], w_idx, idx_k_norm[0])
    q_rows = q_s.reshape(nseq, ntok, N_HEADS, HEAD_DIM).transpose(0, 2, 1, 3).reshape(
        nseq, N_HEADS * ntok, HEAD_DIM)
    qi2d = qi_s.reshape(nseq, ntok, IDX_HEADS, IDX_DIM).transpose(0, 2, 1, 3).reshape(
        nseq, IDX_HEADS * ntok, IDX_DIM)
    wcol = wi_s.reshape(nseq, ntok, IDX_HEADS).transpose(0, 2, 1).reshape(nseq, IDX_HEADS * ntok, 1)

    def pad_rows(a, rows):
        a = a.reshape(nseq, -1, a.shape[-1])
        return jnp.pad(a, ((0, 0), (0, rows - a.shape[1]), (0, 0)))

    bias = _select_decode_call(page_table, qi2d, wcol, pad_rows(ki_s, PAGE_SIZE).transpose(0, 2, 1),
                               cache_kidx[0].transpose(0, 2, 1), pages_per_step=math.gcd(n_pages, 16))
    slope_col = jnp.repeat(_alibi_slopes(), ntok).reshape(N_HEADS * ntok, 1)
    out_a_s = _attn_decode_call(page_table, q_rows, slope_col, bias,
                                pad_rows(kb_s.reshape(m_s * N_HEADS, HEAD_DIM), PAGE_ROWS),
                                pad_rows(v_s.astype(BF16).reshape(m_s * N_HEADS, HEAD_DIM), PAGE_ROWS),
                                cache_k.reshape(n_pool, PAGE_ROWS, HEAD_DIM),
                                cache_v.reshape(n_pool, PAGE_ROWS, HEAD_DIM),
                                pages_per_step=math.gcd(n_pages, 16))
    out_a_s = out_a_s.reshape(nseq, N_HEADS, ntok, HEAD_DIM).transpose(0, 2, 1, 3).reshape(m_s, w_attn)
    ws_s = jnp.einsum("ab,gts->gatbs", jnp.eye(nseq, dtype=F32), w_spatial[0][:, :ntok, :ntok]).reshape(
        N_GROUPS, m_s, m_s)
    bs_s = jnp.tile(b_spatial[0][:, :ntok].T, (nseq, 1))
    y_s, *w2_bf16 = finish(h_s, out_a_s, u_s, vb_s, rest_s, (w2_gate[0], w2_up[0], w2_down[0]), ws_s, bs_s,
                           m_s, m_s, 1, 1, m_s, True)

    xp = x_prompt.reshape(batch * seq, d)
    tf_p, tp_p = seq // tm_ffn, seq // tq
    h_p, proj_p, rest_p, _ = stream(xp, mods_p, w1_bf16, tm_ffn, tq, tf_p, tp_p, False)
    qT, qiT, wT, vT, k_p, kb_p, v_p, ki_p, kib_p, u_p, vb_p = proj_p
    out_a_p = _attn_prompt_call(qT, qiT, wT, kb_p, vT, kib_p, batch=batch, seq=seq, tq=tq)
    tm_out = next(t for t in (512, 256) if seq % t == 0)
    y_p = finish(h_p, out_a_p, u_p, vb_p, rest_p, w2_bf16, w_spatial[0], b_spatial[0].T,
                 tm_ffn, tm_out, tf_p, seq // tm_out, CHUNK, False)

    return (
        y_p.reshape(batch, seq, d),
        y_s.reshape(nseq, ntok, d),
        k_p.reshape(1, batch, seq, N_HEADS, HEAD_DIM),
        v_p.reshape(1, batch, seq, N_HEADS, HEAD_DIM),
        ki_p.reshape(1, batch, seq, IDX_DIM),
        k_s.reshape(1, nseq, ntok, N_HEADS, HEAD_DIM),
        v_s.reshape(1, nseq, ntok, N_HEADS, HEAD_DIM),
        ki_s.reshape(1, nseq, ntok, IDX_DIM),
        vb_s.reshape(1, nseq, ntok, w_g),
    )


def kernel(x_prompt, x_sample, cache_k, cache_v, cache_kidx, page_table, c_prompt, c_sample, w_ada, b_ada, norm_ffn1, w1_gate, w1_up, w1_down, norm_mix, w_in, q_norm, k_norm, idx_k_norm, w_spatial, b_spatial, v_norm, out_norm_a, out_norm_b, w_out, norm_ffn2, w2_gate, w2_up, w2_down):
    return _forward(x_prompt, x_sample, cache_k, cache_v, cache_kidx, page_table, c_prompt, c_sample,
                    w_ada, b_ada, norm_ffn1, w1_gate, w1_up, w1_down, norm_mix, w_in, q_norm, k_norm,
                    idx_k_norm, w_spatial, b_spatial, v_norm, out_norm_a, out_norm_b, w_out, norm_ffn2,
                    w2_gate, w2_up, w2_down)
```
